```python
import jax, jax.numpy as jnp
from jax import lax
import numpy as np

D_MODEL = 1024
BATCH = 8
SEQ = 2048
DEPTH = 2
DEC_BATCH = 128
DEC_SEQ = 8
PAST_LEN = 16384
PAGE_SIZE = 128

PLE_DIM = 256
W_POOL = 512
POOL_WINDOWS = (2, 4, 8, 16)
N_POOL_GROUPS = 4
POOL_GROUP = W_POOL // N_POOL_GROUPS
POOL_BUF = max(POOL_WINDOWS) - 1
W_SCONV = 512
SCONV_K = 3
W_CCONV = 512
CCONV_K = 31
D_FF = 2816
N_BRANCH = 3
IN_COLS = W_POOL + 3 * W_SCONV + 2 * W_CCONV + N_BRANCH * D_MODEL
EPS = 1e-6

kernel_name = "hybrid_pool_shortconv_conformer_decoder_step"


def rmsnorm(x, g):
    xf = x.astype(jnp.float32)
    y = xf * lax.rsqrt(jnp.mean(xf * xf, axis=-1, keepdims=True) + EPS)
    return (y * g.astype(jnp.float32)).astype(x.dtype)


def layernorm(x, g, b):
    xf = x.astype(jnp.float32)
    mu = jnp.mean(xf, axis=-1, keepdims=True)
    xc = xf - mu
    y = xc * lax.rsqrt(jnp.mean(xc * xc, axis=-1, keepdims=True) + EPS)
    return (y * g.astype(jnp.float32) + b.astype(jnp.float32)).astype(x.dtype)


def swiglu(x, wg, wu, wd):
    return (jax.nn.silu(x @ wg) * (x @ wu)) @ wd


def causal_dwconv(buf, x, w):
    xp = jnp.concatenate([buf.astype(x.dtype), x], axis=1)
    c = x.shape[-1]
    out = lax.conv_general_dilated(xp, w[:, None, :].astype(x.dtype), window_strides=(1,), padding='VALID',
                                   dimension_numbers=('NWC', 'WIO', 'NWC'), feature_group_count=c)
    k = w.shape[0]
    return out, xp[:, xp.shape[1] - (k - 1):]


def pool_mixer(buf, a, pos0, pool_w, pool_scale):
    n, l, _ = a.shape
    ap = jnp.concatenate([buf.astype(a.dtype), a], axis=1)
    cs = jnp.pad(jnp.cumsum(ap.astype(jnp.float32), axis=1), ((0, 0), (1, 0), (0, 0)))
    pos = pos0 + jnp.arange(l)
    means = []
    for gi, w in enumerate(POOL_WINDOWS):
        sl = slice(gi * POOL_GROUP, (gi + 1) * POOL_GROUP)
        s = cs[:, POOL_BUF + 1:POOL_BUF + 1 + l, sl] - cs[:, POOL_BUF + 1 - w:POOL_BUF + 1 - w + l, sl]
        cnt = jnp.minimum(w, pos + 1).astype(jnp.float32)
        means.append(s / cnt[None, :, None])
    m = (jnp.concatenate(means, axis=-1) - a.astype(jnp.float32)).astype(a.dtype)
    m = m.reshape(n, l, N_POOL_GROUPS, POOL_GROUP)
    y = jnp.einsum('blgc,gcd->blgd', m, pool_w).reshape(n, l, W_POOL)
    return y * pool_scale, ap[:, ap.shape[1] - POOL_BUF:]


def mixer(u, bpool, bsc, bcf, pos0, P, i):
    n, l, _ = u.shape
    z = u @ P['w_in'][i]
    o = 0
    a = z[..., o:o + W_POOL]; o += W_POOL
    v = z[..., o:o + W_SCONV]; o += W_SCONV
    bg = z[..., o:o + W_SCONV]; o += W_SCONV
    cg = z[..., o:o + W_SCONV]; o += W_SCONV
    ga = z[..., o:o + W_CCONV]; o += W_CCONV
    gg = z[..., o:o + W_CCONV]; o += W_CCONV
    gates = jax.nn.sigmoid(z[..., o:].reshape(n, l, N_BRANCH, D_MODEL))
    ya, nb_pool = pool_mixer(bpool, a, pos0, P['pool_w'][i], P['pool_scale'][i])
    cv, nb_sc = causal_dwconv(bsc, cg * v, P['sc_conv'][i])
    yb = bg * cv
    glu = ga * jax.nn.sigmoid(gg)
    cc, nb_cf = causal_dwconv(bcf, glu, P['cf_conv'][i])
    yc = jax.nn.silu(layernorm(cc + P['cf_conv_b'][i], P['cf_ln_g'][i], P['cf_ln_b'][i]))
    m = (gates[:, :, 0] * (ya @ P['w_pool_out'][i])
         + gates[:, :, 1] * (yb @ P['w_sc_out'][i])
         + gates[:, :, 2] * (yc @ P['w_cf_out'][i]))
    return m @ P['w_o'][i], nb_pool, nb_sc, nb_cf


def trunk(x, p, bpool, bsc, bcf, pos0, P):
    h = x
    npool, nsc, ncf = [], [], []
    for i in range(DEPTH):
        f = swiglu(rmsnorm(h, P['g_f1_pre'][i]), P['w_f1_gate'][i], P['w_f1_up'][i], P['w_f1_down'][i])
        h = h + 0.5 * rmsnorm(f, P['g_f1_post'][i])
        mo, a1, a2, a3 = mixer(rmsnorm(h, P['g_mix_pre'][i]), bpool[i], bsc[i], bcf[i], pos0, P, i)
        h = h + rmsnorm(mo, P['g_mix_post'][i])
        f = swiglu(rmsnorm(h, P['g_f2_pre'][i]), P['w_f2_gate'][i], P['w_f2_up'][i], P['w_f2_down'][i])
        h = h + 0.5 * rmsnorm(f, P['g_f2_post'][i])
        gate = jax.nn.sigmoid(rmsnorm(h, P['g_ple_pre'][i]) @ P['w_ple_gate'][i])
        h = h + rmsnorm((p[i] @ P['w_ple_proj'][i]) * gate, P['g_ple_post'][i])
        npool.append(a1); nsc.append(a2); ncf.append(a3)
    return rmsnorm(h, P['g_final']), jnp.stack(npool), jnp.stack(nsc), jnp.stack(ncf)


def setup_inputs(seed: int = 0) -> dict:
    key = jax.random.key(seed)
    ks = iter(jax.random.split(key, 64))
    f32 = jnp.float32

    def nrm(shape, scale=1.0):
        return jax.random.normal(next(ks), shape, f32) * scale

    def gain(shape):
        return 1.0 + 0.05 * jax.random.normal(next(ks), shape, f32)

    d = {}
    d['x_prompt'] = nrm((BATCH, SEQ, D_MODEL))
    d['x_sample'] = nrm((DEC_BATCH, DEC_SEQ, D_MODEL))
    d['p_prompt'] = nrm((DEPTH, BATCH, SEQ, PLE_DIM))
    d['p_sample'] = nrm((DEPTH, DEC_BATCH, DEC_SEQ, PLE_DIM))
    d['state_pool'] = nrm((DEPTH, DEC_BATCH, POOL_BUF, W_POOL))
    d['state_sconv'] = nrm((DEPTH, DEC_BATCH, SCONV_K - 1, W_SCONV))
    d['state_cconv'] = nrm((DEPTH, DEC_BATCH, CCONV_K - 1, W_CCONV), 0.5)
    d['g_f1_pre'] = gain((DEPTH, D_MODEL))
    d['w_f1_gate'] = nrm((DEPTH, D_MODEL, D_FF), D_MODEL ** -0.5)
    d['w_f1_up'] = nrm((DEPTH, D_MODEL, D_FF), D_MODEL ** -0.5)
    d['w_f1_down'] = nrm((DEPTH, D_FF, D_MODEL), D_FF ** -0.5)
    d['g_f1_post'] = gain((DEPTH, D_MODEL))
    d['g_mix_pre'] = gain((DEPTH, D_MODEL))
    d['w_in'] = nrm((DEPTH, D_MODEL, IN_COLS), D_MODEL ** -0.5)
    d['pool_w'] = nrm((DEPTH, N_POOL_GROUPS, POOL_GROUP, POOL_GROUP), POOL_GROUP ** -0.5)
    d['pool_scale'] = gain((DEPTH, W_POOL))
    d['w_pool_out'] = nrm((DEPTH, W_POOL, D_MODEL), W_POOL ** -0.5)
    d['sc_conv'] = nrm((DEPTH, SCONV_K, W_SCONV), SCONV_K ** -0.5)
    d['w_sc_out'] = nrm((DEPTH, W_SCONV, D_MODEL), W_SCONV ** -0.5)
    d['cf_conv'] = nrm((DEPTH, CCONV_K, W_CCONV), CCONV_K ** -0.5)
    d['cf_conv_b'] = nrm((DEPTH, W_CCONV), 0.01)
    d['cf_ln_g'] = gain((DEPTH, W_CCONV))
    d['cf_ln_b'] = nrm((DEPTH, W_CCONV), 0.01)
    d['w_cf_out'] = nrm((DEPTH, W_CCONV, D_MODEL), W_CCONV ** -0.5)
    d['w_o'] = nrm((DEPTH, D_MODEL, D_MODEL), D_MODEL ** -0.5)
    d['g_mix_post'] = gain((DEPTH, D_MODEL))
    d['g_f2_pre'] = gain((DEPTH, D_MODEL))
    d['w_f2_gate'] = nrm((DEPTH, D_MODEL, D_FF), D_MODEL ** -0.5)
    d['w_f2_up'] = nrm((DEPTH, D_MODEL, D_FF), D_MODEL ** -0.5)
    d['w_f2_down'] = nrm((DEPTH, D_FF, D_MODEL), D_FF ** -0.5)
    d['g_f2_post'] = gain((DEPTH, D_MODEL))
    d['g_ple_pre'] = gain((DEPTH, D_MODEL))
    d['w_ple_gate'] = nrm((DEPTH, D_MODEL, D_MODEL), D_MODEL ** -0.5)
    d['w_ple_proj'] = nrm((DEPTH, PLE_DIM, D_MODEL), PLE_DIM ** -0.5)
    d['g_ple_post'] = gain((DEPTH, D_MODEL))
    d['g_final'] = gain((D_MODEL,))
    return d


def reference(x_prompt, x_sample, p_prompt, p_sample, state_pool, state_sconv, state_cconv,
              g_f1_pre, w_f1_gate, w_f1_up, w_f1_down, g_f1_post,
              g_mix_pre, w_in, pool_w, pool_scale, w_pool_out, sc_conv, w_sc_out,
              cf_conv, cf_conv_b, cf_ln_g, cf_ln_b, w_cf_out, w_o, g_mix_post,
              g_f2_pre, w_f2_gate, w_f2_up, w_f2_down, g_f2_post,
              g_ple_pre, w_ple_gate, w_ple_proj, g_ple_post, g_final):
    P = dict(g_f1_pre=g_f1_pre, w_f1_gate=w_f1_gate, w_f1_up=w_f1_up, w_f1_down=w_f1_down, g_f1_post=g_f1_post,
             g_mix_pre=g_mix_pre, w_in=w_in, pool_w=pool_w, pool_scale=pool_scale, w_pool_out=w_pool_out,
             sc_conv=sc_conv, w_sc_out=w_sc_out, cf_conv=cf_conv, cf_conv_b=cf_conv_b, cf_ln_g=cf_ln_g,
             cf_ln_b=cf_ln_b, w_cf_out=w_cf_out, w_o=w_o, g_mix_post=g_mix_post,
             g_f2_pre=g_f2_pre, w_f2_gate=w_f2_gate, w_f2_up=w_f2_up, w_f2_down=w_f2_down, g_f2_post=g_f2_post,
             g_ple_pre=g_ple_pre, w_ple_gate=w_ple_gate, w_ple_proj=w_ple_proj, g_ple_post=g_ple_post,
             g_final=g_final)
    dt = x_prompt.dtype
    zp = jnp.zeros((DEPTH, BATCH, POOL_BUF, W_POOL), dt)
    zs = jnp.zeros((DEPTH, BATCH, SCONV_K - 1, W_SCONV), dt)
    zc = jnp.zeros((DEPTH, BATCH, CCONV_K - 1, W_CCONV), dt)
    y_prompt, npool_p, nsc_p, ncf_p = trunk(x_prompt, p_prompt, zp, zs, zc, 0, P)
    y_sample, npool_s, nsc_s, ncf_s = trunk(x_sample, p_sample, state_pool, state_sconv, state_cconv, PAST_LEN, P)
    return (y_prompt, y_sample, npool_p, nsc_p, ncf_p, npool_s, nsc_s, ncf_s)
```

```python
import functools

import jax
import jax.numpy as jnp
from jax import lax
from jax.experimental import pallas as pl
from jax.experimental.pallas import tpu as pltpu

D_MODEL = 1024
D_FF = 2816
PLE_DIM = 256
W_BR = 512
POOL_WINDOWS = (2, 4, 8, 16)
POOL_BUF = 15
SCONV_K = 3
CCONV_K = 31
PAST_LEN = 16384
EPS = 1e-6

LANES = 128
SUBLANES = 8
N_SLAB = W_BR // LANES
CONV_ROWS = 64

C_A, C_V, C_BG, C_CG, C_GA, C_GG, C_G0 = 0, 512, 1024, 1536, 2048, 2560, 3072
IN_COLS = C_G0 + 3 * D_MODEL

FF_CHUNKS = ((0, 512), (512, 512), (1024, 512), (1536, 512), (2048, 512), (2560, 256))
MIX_CHUNK = 512

VMEM_LIMIT = 56 * 1024 * 1024

f32 = jnp.float32
bf16 = jnp.bfloat16


def _round_up(x, m):
    return (x + m - 1) // m * m


def _rms(x, g):
    return x * lax.rsqrt(jnp.mean(x * x, axis=-1, keepdims=True) + EPS) * g


def _dot(a, b):
    return jnp.dot(a, b, preferred_element_type=f32)


def _resident(shape):
    nd = len(shape)
    return pl.BlockSpec(shape, lambda *_: (0,) * nd, pipeline_mode=pl.Buffered(1))


def _swiglu_into(acc_scr, u_scr, wg_ref, wu_ref, wd_ref):
    for n, (c0, cw) in enumerate(FF_CHUNKS):
        u = u_scr[...]
        g = _dot(u, wg_ref[:, c0:c0 + cw])
        up = _dot(u, wu_ref[:, c0:c0 + cw])
        a = (jax.nn.silu(g) * up).astype(bf16)
        d = _dot(a, wd_ref[c0:c0 + cw, :])
        if n == 0:
            acc_scr[...] = d
        else:
            acc_scr[...] += d


def _ffn_kernel(h_ref, gpre_ref, wg_ref, wu_ref, wd_ref, gpost_ref, o_ref, u_scr, acc_scr):
    u_scr[...] = _rms(h_ref[...], gpre_ref[...]).astype(bf16)
    _swiglu_into(acc_scr, u_scr, wg_ref, wu_ref, wd_ref)
    o_ref[...] = h_ref[...] + 0.5 * _rms(acc_scr[...], gpost_ref[...])


def _ffn_ple_kernel(h_ref, p_ref, gpre_ref, wg_ref, wu_ref, wd_ref, gpost_ref,
                    gppre_ref, wpg_ref, wpp_ref, gppost_ref, gfin_ref, o_ref, u_scr, acc_scr, *, final):
    u_scr[...] = _rms(h_ref[...], gpre_ref[...]).astype(bf16)
    _swiglu_into(acc_scr, u_scr, wg_ref, wu_ref, wd_ref)
    h = h_ref[...] + 0.5 * _rms(acc_scr[...], gpost_ref[...])
    gate = jax.nn.sigmoid(_dot(_rms(h, gppre_ref[...]).astype(bf16), wpg_ref[...]))
    proj = _dot(p_ref[...].astype(bf16), wpp_ref[...])
    h = h + _rms(proj * gate, gppost_ref[...])
    if final:
        h = _rms(h, gfin_ref[...])
    o_ref[...] = h


def _ffn_call(h, gpre, wg, wu, wd, gpost, *, tm):
    t = h.shape[0]
    row = pl.BlockSpec((tm, D_MODEL), lambda i: (i, 0))
    return pl.pallas_call(
        _ffn_kernel,
        grid=(t // tm,),
        in_specs=[row, _resident(gpre.shape), _resident(wg.shape), _resident(wu.shape), _resident(wd.shape),
                  _resident(gpost.shape)],
        out_specs=row,
        out_shape=jax.ShapeDtypeStruct(h.shape, f32),
        scratch_shapes=[pltpu.VMEM((tm, D_MODEL), bf16), pltpu.VMEM((tm, D_MODEL), f32)],
        compiler_params=pltpu.CompilerParams(dimension_semantics=("arbitrary",), vmem_limit_bytes=VMEM_LIMIT),
        name="ffn",
    )(h, gpre, wg, wu, wd, gpost)


def _ffn_ple_call(h, p, gpre, wg, wu, wd, gpost, gppre, wpg, wpp, gppost, gfin, *, tm, final):
    t = h.shape[0]
    row = pl.BlockSpec((tm, D_MODEL), lambda i: (i, 0))
    prow = pl.BlockSpec((tm, PLE_DIM), lambda i: (i, 0))
    params = (gpre, wg, wu, wd, gpost, gppre, wpg, wpp, gppost, gfin)
    return pl.pallas_call(
        functools.partial(_ffn_ple_kernel, final=final),
        grid=(t // tm,),
        in_specs=[row, prow] + [_resident(a.shape) for a in params],
        out_specs=row,
        out_shape=jax.ShapeDtypeStruct(h.shape, f32),
        scratch_shapes=[pltpu.VMEM((tm, D_MODEL), bf16), pltpu.VMEM((tm, D_MODEL), f32)],
        compiler_params=pltpu.CompilerParams(dimension_semantics=("arbitrary",), vmem_limit_bytes=VMEM_LIMIT),
        name="ffn_ple",
    )(h, p, *params)


OFF_POOL = _round_up(POOL_BUF, SUBLANES)
OFF_SC = _round_up(SCONV_K - 1, SUBLANES)
OFF_CF = _round_up(CCONV_K - 1, SUBLANES)


def _conv_blocks(ns, l):
    if ns == 1:
        return [(0, l0, 1, CONV_ROWS) for l0 in range(0, l, CONV_ROWS)]
    sb = CONV_ROWS // l
    return [(s0, 0, sb, l) for s0 in range(0, ns, sb)]


def _mixer_body(h_ref, o_ref, pos0, ns, l,
                gpre_ref, win_ref, poolw_ref, pscale_ref, wpo_ref, scw_ref, wso_ref, cfw_ref, cfb_ref,
                lng_ref, lnb_ref, wco_ref, wo_ref, gpost_ref,
                u_scr, z_scr, y_scr, m_scr, cc_scr, xp_pool, xp_sc, xp_cf):
    tm = ns * l
    blocks = _conv_blocks(ns, l)

    def to_slab(xp, off, j, val):
        xp[j, :, off:off + l, :] = val.reshape(ns, l, LANES)

    u_scr[...] = _rms(h_ref[...], gpre_ref[...]).astype(bf16)

    z_scr[:, 0:W_BR] = _dot(u_scr[...], win_ref[:, C_A:C_A + W_BR])
    for j in range(N_SLAB):
        to_slab(xp_pool, OFF_POOL, j, z_scr[:, j * LANES:(j + 1) * LANES])
    for j, w in enumerate(POOL_WINDOWS):
        for (s0, l0, sb, lb) in blocks:
            r0 = s0 * l + l0
            cur = xp_pool[j, s0:s0 + sb, OFF_POOL + l0:OFF_POOL + l0 + lb, :]
            s = cur
            for i in range(1, w):
                s = s + xp_pool[j, s0:s0 + sb, OFF_POOL + l0 - i:OFF_POOL + l0 - i + lb, :]
            if pos0 is not None and l0 < w - 1:
                pos = pos0 + l0 + lax.broadcasted_iota(jnp.int32, (sb, lb, LANES), 1)
                mean = s / jnp.minimum(w, pos + 1).astype(f32)
            else:
                mean = s * (1.0 / w)
            mm = (mean - cur).reshape(sb * lb, LANES).astype(bf16)
            ya = _dot(mm, poolw_ref[j]) * pscale_ref[:, j * LANES:(j + 1) * LANES]
            y_scr[r0:r0 + sb * lb, j * LANES:(j + 1) * LANES] = ya.astype(bf16)

    z_scr[...] = _dot(u_scr[...], win_ref[:, C_V:C_V + 3 * W_BR])
    for j in range(N_SLAB):
        cs = slice(j * LANES, (j + 1) * LANES)
        v = z_scr[:, cs]
        cg = z_scr[:, 2 * W_BR + j * LANES:2 * W_BR + (j + 1) * LANES]
        to_slab(xp_sc, OFF_SC, j, cg * v)
    for j in range(N_SLAB):
        taps = [scw_ref[k:k + 1, j * LANES:(j + 1) * LANES].reshape(1, 1, LANES) for k in range(SCONV_K)]
        for (s0, l0, sb, lb) in blocks:
            r0 = s0 * l + l0
            cv = None
            for k in range(SCONV_K):
                st = OFF_SC - (SCONV_K - 1) + k + l0
                term = xp_sc[j, s0:s0 + sb, st:st + lb, :] * taps[k]
                cv = term if cv is None else cv + term
            bg = z_scr[r0:r0 + sb * lb, W_BR + j * LANES:W_BR + (j + 1) * LANES]
            y_scr[r0:r0 + sb * lb, W_BR + j * LANES:W_BR + (j + 1) * LANES] = (
                bg * cv.reshape(sb * lb, LANES)).astype(bf16)

    z_scr[:, 0:2 * W_BR] = _dot(u_scr[...], win_ref[:, C_GA:C_GA + 2 * W_BR])
    for j in range(N_SLAB):
        ga = z_scr[:, j * LANES:(j + 1) * LANES]
        gg = z_scr[:, W_BR + j * LANES:W_BR + (j + 1) * LANES]
        to_slab(xp_cf, OFF_CF, j, ga * jax.nn.sigmoid(gg))
    for j in range(N_SLAB):
        cs = slice(j * LANES, (j + 1) * LANES)
        taps = [cfw_ref[k:k + 1, cs].reshape(1, 1, LANES) for k in range(CCONV_K)]
        bias = cfb_ref[:, cs]
        for (s0, l0, sb, lb) in blocks:
            r0 = s0 * l + l0
            cc = None
            for k in range(CCONV_K):
                st = OFF_CF - (CCONV_K - 1) + k + l0
                term = xp_cf[j, s0:s0 + sb, st:st + lb, :] * taps[k]
                cc = term if cc is None else cc + term
            cc_scr[r0:r0 + sb * lb, cs] = cc.reshape(sb * lb, LANES) + bias
    x = cc_scr[...]
    xc = x - jnp.mean(x, axis=-1, keepdims=True)
    ln = xc * lax.rsqrt(jnp.mean(xc * xc, axis=-1, keepdims=True) + EPS) * lng_ref[...] + lnb_ref[...]
    y_scr[:, 2 * W_BR:3 * W_BR] = jax.nn.silu(ln).astype(bf16)

    for c0 in range(0, D_MODEL, MIX_CHUNK):
        cs = slice(c0, c0 + MIX_CHUNK)
        acc = None
        for b, w_ref in enumerate((wpo_ref, wso_ref, wco_ref)):
            gate = jax.nn.sigmoid(_dot(u_scr[...], win_ref[:, C_G0 + b * D_MODEL + c0:C_G0 + b * D_MODEL + c0 + MIX_CHUNK]))
            term = gate * _dot(y_scr[:, b * W_BR:(b + 1) * W_BR], w_ref[:, cs])
            acc = term if acc is None else acc + term
        m_scr[:, cs] = acc.astype(bf16)
    mo = _dot(m_scr[...], wo_ref[...])
    o_ref[...] = h_ref[...] + _rms(mo, gpost_ref[...])


def _mixer_prompt_kernel(h_ref, *refs, l):
    params = refs[:14]
    o_ref, npool_ref, nsc_ref, ncf_ref = refs[14:18]
    scr = refs[18:]
    xp_pool, xp_sc, xp_cf = scr[5:8]
    c = pl.program_id(1)
    carried = ((xp_pool, OFF_POOL, POOL_BUF, npool_ref), (xp_sc, OFF_SC, SCONV_K - 1, nsc_ref),
               (xp_cf, OFF_CF, CCONV_K - 1, ncf_ref))

    @pl.when(c == 0)
    def _():
        for xp, off, _, _ in carried:
            xp[:, :, 0:off, :] = jnp.zeros((N_SLAB, 1, off, LANES), f32)

    _mixer_body(h_ref, o_ref, c * l, 1, l, *params, *scr)

    for xp, off, nb, _ in carried:
        xp[:, :, off - nb:off, :] = xp[:, :, off + l - nb:off + l, :]

    @pl.when(c == pl.num_programs(1) - 1)
    def _():
        for xp, off, nb, out_ref in carried:
            for j in range(N_SLAB):
                out_ref[0, :, j * LANES:(j + 1) * LANES] = xp[j, 0, off - nb:off, :]


def _mixer_sample_kernel(h_ref, spool_ref, ssc_ref, scf_ref, *refs, ns, l):
    params = refs[:14]
    o_ref, npool_ref, nsc_ref, ncf_ref = refs[14:18]
    scr = refs[18:]
    xp_pool, xp_sc, xp_cf = scr[5:8]
    carried = ((xp_pool, OFF_POOL, POOL_BUF, spool_ref, npool_ref), (xp_sc, OFF_SC, SCONV_K - 1, ssc_ref, nsc_ref),
               (xp_cf, OFF_CF, CCONV_K - 1, scf_ref, ncf_ref))
    for xp, off, nb, in_ref, _ in carried:
        for j in range(N_SLAB):
            xp[j, :, off - nb:off, :] = in_ref[:, :, j * LANES:(j + 1) * LANES]
    assert PAST_LEN + 1 >= max(POOL_WINDOWS)
    _mixer_body(h_ref, o_ref, None, ns, l, *params, *scr)
    for xp, off, nb, _, out_ref in carried:
        for j in range(N_SLAB):
            out_ref[:, :, j * LANES:(j + 1) * LANES] = xp[j, :, off + l - nb:off + l, :]


def _mixer_scratch(ns, l):
    tm = ns * l
    return [
        pltpu.VMEM((tm, D_MODEL), bf16),
        pltpu.VMEM((tm, 3 * W_BR), f32),
        pltpu.VMEM((tm, 3 * W_BR), bf16),
        pltpu.VMEM((tm, D_MODEL), bf16),
        pltpu.VMEM((tm, W_BR), f32),
        pltpu.VMEM((N_SLAB, ns, OFF_POOL + l, LANES), f32),
        pltpu.VMEM((N_SLAB, ns, OFF_SC + l, LANES), f32),
        pltpu.VMEM((N_SLAB, ns, OFF_CF + l, LANES), f32),
    ]


def _mixer_prompt_call(h, params, *, batch, seq, l):
    ch = seq // l
    row = pl.BlockSpec((l, D_MODEL), lambda b, c: (b * ch + c, 0))

    def state_spec(nb):
        return pl.BlockSpec((1, nb, W_BR), lambda b, c: (b, 0, 0))

    return pl.pallas_call(
        functools.partial(_mixer_prompt_kernel, l=l),
        grid=(batch, ch),
        in_specs=[row] + [_resident(a.shape) for a in params],
        out_specs=[row, state_spec(POOL_BUF), state_spec(SCONV_K - 1), state_spec(CCONV_K - 1)],
        out_shape=[jax.ShapeDtypeStruct(h.shape, f32),
                   jax.ShapeDtypeStruct((batch, POOL_BUF, W_BR), f32),
                   jax.ShapeDtypeStruct((batch, SCONV_K - 1, W_BR), f32),
                   jax.ShapeDtypeStruct((batch, CCONV_K - 1, W_BR), f32)],
        scratch_shapes=_mixer_scratch(1, l),
        compiler_params=pltpu.CompilerParams(dimension_semantics=("arbitrary", "arbitrary"),
                                             vmem_limit_bytes=VMEM_LIMIT),
        name="mixer_prompt",
    )(h, *params)


def _mixer_sample_call(h, spool, ssc, scf, params, *, nseq, l, ns):
    row = pl.BlockSpec((ns * l, D_MODEL), lambda s: (s, 0))

    def state_spec(nb):
        return pl.BlockSpec((ns, nb, W_BR), lambda s: (s, 0, 0))

    states = [state_spec(POOL_BUF), state_spec(SCONV_K - 1), state_spec(CCONV_K - 1)]
    return pl.pallas_call(
        functools.partial(_mixer_sample_kernel, ns=ns, l=l),
        grid=(nseq // ns,),
        in_specs=[row] + states + [_resident(a.shape) for a in params],
        out_specs=[row] + states,
        out_shape=[jax.ShapeDtypeStruct(h.shape, f32),
                   jax.ShapeDtypeStruct(spool.shape, f32),
                   jax.ShapeDtypeStruct(ssc.shape, f32),
                   jax.ShapeDtypeStruct(scf.shape, f32)],
        scratch_shapes=_mixer_scratch(ns, l),
        compiler_params=pltpu.CompilerParams(dimension_semantics=("arbitrary",), vmem_limit_bytes=VMEM_LIMIT),
        name="mixer_sample",
    )(h, spool, ssc, scf, *params)


def kernel(x_prompt, x_sample, p_prompt, p_sample, state_pool, state_sconv, state_cconv, g_f1_pre, w_f1_gate, w_f1_up, w_f1_down, g_f1_post, g_mix_pre, w_in, pool_w, pool_scale, w_pool_out, sc_conv, w_sc_out, cf_conv, cf_conv_b, cf_ln_g, cf_ln_b, w_cf_out, w_o, g_mix_post, g_f2_pre, w_f2_gate, w_f2_up, w_f2_down, g_f2_post, g_ple_pre, w_ple_gate, w_ple_proj, g_ple_post, g_final):
    depth = w_in.shape[0]
    batch, seq, _ = x_prompt.shape
    nseq, dec_seq, _ = x_sample.shape
    tm = 512
    ns = 32

    def vec(a, i):
        return a[i].reshape(1, -1)

    def mat(a, i):
        return a[i].astype(bf16)

    hp = x_prompt.reshape(batch * seq, D_MODEL)
    hs = x_sample.reshape(nseq * dec_seq, D_MODEL)
    pp = p_prompt.reshape(depth, batch * seq, PLE_DIM)
    ps = p_sample.reshape(depth, nseq * dec_seq, PLE_DIM)
    gfin = g_final.reshape(1, -1)
    new_p, new_s = [], []
    for i in range(depth):
        f1 = (vec(g_f1_pre, i), mat(w_f1_gate, i), mat(w_f1_up, i), mat(w_f1_down, i), vec(g_f1_post, i))
        hp = _ffn_call(hp, *f1, tm=tm)
        hs = _ffn_call(hs, *f1, tm=tm)
        mix = (vec(g_mix_pre, i), mat(w_in, i), mat(pool_w, i), vec(pool_scale, i), mat(w_pool_out, i),
               sc_conv[i], mat(w_sc_out, i), cf_conv[i], vec(cf_conv_b, i), vec(cf_ln_g, i), vec(cf_ln_b, i),
               mat(w_cf_out, i), mat(w_o, i), vec(g_mix_post, i))
        hp, *st_p = _mixer_prompt_call(hp, mix, batch=batch, seq=seq, l=tm)
        hs, *st_s = _mixer_sample_call(hs, state_pool[i], state_sconv[i], state_cconv[i], mix,
                                       nseq=nseq, l=dec_seq, ns=ns)
        new_p.append(st_p)
        new_s.append(st_s)
        f2 = (vec(g_f2_pre, i), mat(w_f2_gate, i), mat(w_f2_up, i), mat(w_f2_down, i), vec(g_f2_post, i),
              vec(g_ple_pre, i), mat(w_ple_gate, i), mat(w_ple_proj, i), vec(g_ple_post, i), gfin)
        final = i == depth - 1
        hp = _ffn_ple_call(hp, pp[i], *f2, tm=tm, final=final)
        hs = _ffn_ple_call(hs, ps[i], *f2, tm=tm, final=final)
    stack = lambda sts, k: jnp.stack([s[k] for s in sts])
    return (hp.reshape(batch, seq, D_MODEL), hs.reshape(nseq, dec_seq, D_MODEL),
            stack(new_p, 0), stack(new_p, 1), stack(new_p, 2),
            stack(new_s, 0), stack(new_s, 1), stack(new_s, 2))
```

```python
import functools

import jax
import jax.numpy as jnp
from jax import lax
from jax.experimental import pallas as pl
from jax.experimental.pallas import tpu as pltpu

D_MODEL = 1024
D_FF = 2816
PLE_DIM = 256
W_BR = 512
POOL_WINDOWS = (2, 4, 8, 16)
POOL_BUF = 15
SCONV_K = 3
CCONV_K = 31
PAST_LEN = 16384
EPS = 1e-6

LANES = 128
SUBLANES = 8
BF16_ROWS = 16
N_SLAB = W_BR // LANES
CONV_ROWS = 64

C_A, C_V, C_BG, C_CG, C_GA, C_GG, C_G0 = 0, 512, 1024, 1536, 2048, 2560, 3072
IN_COLS = C_G0 + 3 * D_MODEL

FF_CHUNKS = ((0, 512), (512, 512), (1024, 512), (1536, 512), (2048, 512), (2560, 256))
MIX_CHUNK = 512

VMEM_LIMIT = 56 * 1024 * 1024

f32 = jnp.float32
bf16 = jnp.bfloat16


def _round_up(x, m):
    return (x + m - 1) // m * m


def _rms(x, g):
    return x * lax.rsqrt(jnp.mean(x * x, axis=-1, keepdims=True) + EPS) * g


def _dot(a, b):
    return jnp.dot(a, b, preferred_element_type=f32)


def _resident(shape):
    nd = len(shape)
    return pl.BlockSpec(shape, lambda *_: (0,) * nd, pipeline_mode=pl.Buffered(1))


def _layer(shape, i):
    nd = len(shape)
    return pl.BlockSpec((None,) + tuple(shape[1:]), lambda *_: (i,) + (0,) * (nd - 1), pipeline_mode=pl.Buffered(1))


def _params(vmem_limit=VMEM_LIMIT, ndim=1, flags=None):
    return pltpu.CompilerParams(dimension_semantics=("arbitrary",) * ndim, vmem_limit_bytes=vmem_limit, flags=flags)


def _swiglu_into(acc_scr, u_scr, wg_ref, wu_ref, wd_ref):
    for n, (c0, cw) in enumerate(FF_CHUNKS):
        u = u_scr[...]
        g = _dot(u, wg_ref[:, c0:c0 + cw])
        up = _dot(u, wu_ref[:, c0:c0 + cw])
        a = (jax.nn.silu(g) * up).astype(bf16)
        d = _dot(a, wd_ref[c0:c0 + cw, :])
        if n == 0:
            acc_scr[...] = d
        else:
            acc_scr[...] += d


def _ffn_kernel(h_ref, gpre_ref, wg_ref, wu_ref, wd_ref, gpost_ref, o_ref, u_scr, acc_scr):
    u_scr[...] = _rms(h_ref[...], gpre_ref[...]).astype(bf16)
    _swiglu_into(acc_scr, u_scr, wg_ref, wu_ref, wd_ref)
    o_ref[...] = h_ref[...] + 0.5 * _rms(acc_scr[...], gpost_ref[...])


def _ffn_ple_kernel(h_ref, p_ref, gpre_ref, wg_ref, wu_ref, wd_ref, gpost_ref,
                    gppre_ref, wpg_ref, wpp_ref, gppost_ref, gfin_ref, o_ref, u_scr, acc_scr, *, final):
    u_scr[...] = _rms(h_ref[...], gpre_ref[...]).astype(bf16)
    _swiglu_into(acc_scr, u_scr, wg_ref, wu_ref, wd_ref)
    h = h_ref[...] + 0.5 * _rms(acc_scr[...], gpost_ref[...])
    gate = jax.nn.sigmoid(_dot(_rms(h, gppre_ref[...]).astype(bf16), wpg_ref[...]))
    proj = _dot(p_ref[...].astype(bf16), wpp_ref[...])
    h = h + _rms(proj * gate, gppost_ref[...])
    if final:
        h = _rms(h, gfin_ref[...])
    o_ref[...] = h


def _ffn_call(h, i, gpre, wg, wu, wd, gpost, *, tm):
    t = h.shape[0]
    row = pl.BlockSpec((tm, D_MODEL), lambda r: (r, 0))
    params = (gpre, wg, wu, wd, gpost)
    return pl.pallas_call(
        _ffn_kernel,
        grid=(t // tm,),
        in_specs=[row] + [_layer(a.shape, i) for a in params],
        out_specs=row,
        out_shape=jax.ShapeDtypeStruct(h.shape, f32),
        scratch_shapes=[pltpu.VMEM((tm, D_MODEL), bf16), pltpu.VMEM((tm, D_MODEL), f32)],
        compiler_params=_params(),
        name="ffn",
    )(h, *params)


def _ffn_ple_call(h, p, i, gpre, wg, wu, wd, gpost, gppre, wpg, wpp, gppost, gfin, *, tm, final):
    t = h.shape[0]
    row = pl.BlockSpec((tm, D_MODEL), lambda r: (r, 0))
    prow = pl.BlockSpec((None, tm, PLE_DIM), lambda r: (i, r, 0))
    params = (gpre, wg, wu, wd, gpost, gppre, wpg, wpp, gppost)
    return pl.pallas_call(
        functools.partial(_ffn_ple_kernel, final=final),
        grid=(t // tm,),
        in_specs=[row, prow] + [_layer(a.shape, i) for a in params] + [_resident(gfin.shape)],
        out_specs=row,
        out_shape=jax.ShapeDtypeStruct(h.shape, f32),
        scratch_shapes=[pltpu.VMEM((tm, D_MODEL), bf16), pltpu.VMEM((tm, D_MODEL), f32)],
        compiler_params=_params(),
        name="ffn_ple",
    )(h, p, *params, gfin)


OFF_POOL = _round_up(POOL_BUF, SUBLANES)
OFF_SC = _round_up(SCONV_K - 1, SUBLANES)
OFF_CF = _round_up(CCONV_K - 1, SUBLANES)


class _OneSequence:
    def __init__(self, l):
        self.l, self.tm = l, l
        self.blocks = tuple(range(0, l, CONV_ROWS))

    def scratch(self, cols, dtype):
        return pltpu.VMEM((self.l, cols), dtype)

    def slab(self, off):
        return pltpu.VMEM((N_SLAB, off + self.l, LANES), f32)

    def load(self, ref, cs=slice(None)):
        return ref[:, cs]

    def store(self, ref, cs, val):
        ref[:, cs] = val

    def blk_load(self, ref, b, cs):
        return ref[b:b + CONV_ROWS, cs]

    def blk_store(self, ref, b, cs, val):
        ref[b:b + CONV_ROWS, cs] = val

    def win(self, xp, j, start, b):
        return xp[j, start + b:start + b + CONV_ROWS, :]

    def tap(self, w_ref, k, cs):
        return w_ref[k:k + 1, cs]

    def to_slab(self, xp, off, j, val):
        xp[j, off:off + self.l, :] = val

    def blk2d(self, val):
        return val

    def blk_like(self, val2d):
        return val2d

    def row_index(self, b):
        return b + lax.broadcasted_iota(jnp.int32, (CONV_ROWS, LANES), 0)


class _StepMajor:
    def __init__(self, l, ns):
        self.l, self.ns, self.tm = l, ns, l * ns
        self.blocks = tuple(range(0, ns, BF16_ROWS))

    def scratch(self, cols, dtype):
        return pltpu.VMEM((self.l, self.ns, cols), dtype)

    def slab(self, off):
        return pltpu.VMEM((N_SLAB, off + self.l, self.ns, LANES), f32)

    def load(self, ref, cs=slice(None)):
        v = ref[:, :, cs]
        return v.reshape(self.tm, v.shape[-1])

    def store(self, ref, cs, val):
        ref[:, :, cs] = val.reshape(self.l, self.ns, val.shape[-1])

    def blk_load(self, ref, b, cs):
        return ref[:, b:b + BF16_ROWS, cs]

    def blk_store(self, ref, b, cs, val):
        ref[:, b:b + BF16_ROWS, cs] = val

    def win(self, xp, j, start, b):
        return xp[j, start:start + self.l, b:b + BF16_ROWS, :]

    def tap(self, w_ref, k, cs):
        return w_ref[k:k + 1, cs].reshape(1, 1, LANES)

    def to_slab(self, xp, off, j, val):
        xp[j, off:off + self.l, :, :] = val.reshape(self.l, self.ns, LANES)

    def blk2d(self, val):
        return val.reshape(self.l * BF16_ROWS, LANES)

    def blk_like(self, val2d):
        return val2d.reshape(self.l, BF16_ROWS, LANES)


def _mixer_body(lay, read_h, write_o, pos0,
                gpre_ref, win_ref, poolw_ref, pscale_ref, wpo_ref, scw_ref, wso_ref, cfw_ref, cfb_ref,
                lng_ref, lnb_ref, wco_ref, wo_ref, gpost_ref,
                u_scr, zb_scr, g_scr, y_scr, cc_scr, xp_pool, xp_sc, xp_cf):
    def stage(fn):
        fn()

    @stage
    def _project_pool_shortconv():
        u_scr[...] = _rms(read_h(), gpre_ref[...]).astype(bf16)
        zc = _dot(u_scr[...], win_ref[:, C_GA:C_GA + 2 * W_BR])
        for j in range(N_SLAB):
            ga = zc[:, j * LANES:(j + 1) * LANES]
            gg = zc[:, W_BR + j * LANES:W_BR + (j + 1) * LANES]
            lay.to_slab(xp_cf, OFF_CF, j, ga * jax.nn.sigmoid(gg))
        lay.store(zb_scr, slice(None), _dot(u_scr[...], win_ref[:, C_V:C_V + 3 * W_BR]))
        for j in range(N_SLAB):
            v = lay.load(zb_scr, slice(j * LANES, (j + 1) * LANES))
            cg = lay.load(zb_scr, slice(2 * W_BR + j * LANES, 2 * W_BR + (j + 1) * LANES))
            lay.to_slab(xp_sc, OFF_SC, j, cg * v)
        za = _dot(u_scr[...], win_ref[:, C_A:C_A + W_BR])
        for j in range(N_SLAB):
            lay.to_slab(xp_pool, OFF_POOL, j, za[:, j * LANES:(j + 1) * LANES])

        for j in range(N_SLAB):
            cs = slice(j * LANES, (j + 1) * LANES)
            taps = [lay.tap(scw_ref, k, cs) for k in range(SCONV_K)]
            for b in lay.blocks:
                cv = None
                for k in range(SCONV_K):
                    term = lay.win(xp_sc, j, OFF_SC - (SCONV_K - 1) + k, b) * taps[k]
                    cv = term if cv is None else cv + term
                bg = lay.blk_load(zb_scr, b, slice(W_BR + j * LANES, W_BR + (j + 1) * LANES))
                lay.blk_store(y_scr, b, slice(W_BR + j * LANES, W_BR + (j + 1) * LANES), (bg * cv).astype(bf16))

        for j, w in enumerate(POOL_WINDOWS):
            cs = slice(j * LANES, (j + 1) * LANES)
            for b in lay.blocks:
                cur = lay.win(xp_pool, j, OFF_POOL, b)
                s = cur
                for i in range(1, w):
                    s = s + lay.win(xp_pool, j, OFF_POOL - i, b)
                if pos0 is not None and b < w - 1:
                    mean = s / jnp.minimum(w, pos0 + lay.row_index(b) + 1).astype(f32)
                else:
                    mean = s * (1.0 / w)
                mm = lay.blk2d(mean - cur).astype(bf16)
                ya = _dot(mm, poolw_ref[j]) * pscale_ref[:, cs]
                lay.blk_store(y_scr, b, cs, lay.blk_like(ya.astype(bf16)))

    @stage
    def _gates_and_conformer_conv():
        for b in range(3):
            for c0 in range(0, D_MODEL, MIX_CHUNK):
                cs = slice(b * D_MODEL + c0, b * D_MODEL + c0 + MIX_CHUNK)
                g_scr[:, cs] = jax.nn.sigmoid(_dot(u_scr[...], win_ref[:, C_G0 + cs.start:C_G0 + cs.stop]))
        for j in range(N_SLAB):
            cs = slice(j * LANES, (j + 1) * LANES)
            taps = [lay.tap(cfw_ref, k, cs) for k in range(CCONV_K)]
            bias = cfb_ref[:, cs]
            for b in lay.blocks:
                cc = None
                for k in range(CCONV_K):
                    term = lay.win(xp_cf, j, OFF_CF - (CCONV_K - 1) + k, b) * taps[k]
                    cc = term if cc is None else cc + term
                lay.blk_store(cc_scr, b, cs, cc + bias)
        x = lay.load(cc_scr)
        xc = x - jnp.mean(x, axis=-1, keepdims=True)
        ln = xc * lax.rsqrt(jnp.mean(xc * xc, axis=-1, keepdims=True) + EPS) * lng_ref[...] + lnb_ref[...]
        lay.store(y_scr, slice(2 * W_BR, 3 * W_BR), jax.nn.silu(ln).astype(bf16))

    @stage
    def _combine_and_project():
        m = []
        for c0 in range(0, D_MODEL, MIX_CHUNK):
            cs = slice(c0, c0 + MIX_CHUNK)
            acc = None
            for b, w_ref in enumerate((wpo_ref, wso_ref, wco_ref)):
                yb = lay.load(y_scr, slice(b * W_BR, (b + 1) * W_BR))
                term = g_scr[:, b * D_MODEL + c0:b * D_MODEL + c0 + MIX_CHUNK] * _dot(yb, w_ref[:, cs])
                acc = term if acc is None else acc + term
            m.append(acc.astype(bf16))
        mo = _dot(jnp.concatenate(m, axis=-1), wo_ref[...])
        write_o(read_h() + _rms(mo, gpost_ref[...]))


N_MIX_PARAMS = 14


def _mixer_scratch(lay):
    tm = lay.tm
    return [
        pltpu.VMEM((tm, D_MODEL), bf16),
        lay.scratch(3 * W_BR, f32),
        pltpu.VMEM((tm, 3 * D_MODEL), f32),
        lay.scratch(3 * W_BR, bf16),
        lay.scratch(W_BR, f32),
        lay.slab(OFF_POOL), lay.slab(OFF_SC), lay.slab(OFF_CF),
    ]


def _mixer_prompt_kernel(h_ref, *refs, lay):
    params = refs[:N_MIX_PARAMS]
    o_ref, npool_ref, nsc_ref, ncf_ref = refs[N_MIX_PARAMS:N_MIX_PARAMS + 4]
    scr = refs[N_MIX_PARAMS + 4:]
    xp_pool, xp_sc, xp_cf = scr[-3:]
    l = lay.l
    c = pl.program_id(1)
    carried = ((xp_pool, OFF_POOL, POOL_BUF, npool_ref), (xp_sc, OFF_SC, SCONV_K - 1, nsc_ref),
               (xp_cf, OFF_CF, CCONV_K - 1, ncf_ref))

    @pl.when(c == 0)
    def _():
        for xp, off, _, _ in carried:
            xp[:, 0:off, :] = jnp.zeros((N_SLAB, off, LANES), f32)

    def write_o(val):
        o_ref[...] = val

    _mixer_body(lay, lambda: h_ref[...], write_o, c * l, *params, *scr)

    for xp, off, nb, _ in carried:
        xp[:, off - nb:off, :] = xp[:, off + l - nb:off + l, :]

    @pl.when(c == pl.num_programs(1) - 1)
    def _():
        for xp, off, nb, out_ref in carried:
            for j in range(N_SLAB):
                out_ref[0, :, j * LANES:(j + 1) * LANES] = xp[j, off - nb:off, :]


def _mixer_sample_kernel(h_ref, spool_ref, ssc_ref, scf_ref, *refs, lay):
    params = refs[:N_MIX_PARAMS]
    o_ref, npool_ref, nsc_ref, ncf_ref = refs[N_MIX_PARAMS:N_MIX_PARAMS + 4]
    scr = refs[N_MIX_PARAMS + 4:]
    xp_pool, xp_sc, xp_cf = scr[-3:]
    l = lay.l
    carried = ((xp_pool, OFF_POOL, POOL_BUF, spool_ref, npool_ref), (xp_sc, OFF_SC, SCONV_K - 1, ssc_ref, nsc_ref),
               (xp_cf, OFF_CF, CCONV_K - 1, scf_ref, ncf_ref))
    for xp, off, nb, in_ref, _ in carried:
        for j in range(N_SLAB):
            xp[j, off - nb:off, :, :] = in_ref[:, :, j * LANES:(j + 1) * LANES]
    assert PAST_LEN + 1 >= max(POOL_WINDOWS)

    def write_o(val):
        o_ref[...] = val.reshape(l, lay.ns, D_MODEL)

    _mixer_body(lay, lambda: h_ref[...].reshape(lay.tm, D_MODEL), write_o, None, *params, *scr)
    for xp, off, nb, _, out_ref in carried:
        for j in range(N_SLAB):
            out_ref[:, :, j * LANES:(j + 1) * LANES] = xp[j, off + l - nb:off + l, :, :]


def _mixer_prompt_call(h, i, params, *, batch, seq, l):
    ch = seq // l
    lay = _OneSequence(l)
    row = pl.BlockSpec((l, D_MODEL), lambda b, c: (b * ch + c, 0))

    def state_spec(nb):
        return pl.BlockSpec((1, nb, W_BR), lambda b, c: (b, 0, 0))

    return pl.pallas_call(
        functools.partial(_mixer_prompt_kernel, lay=lay),
        grid=(batch, ch),
        in_specs=[row] + [_layer(a.shape, i) for a in params],
        out_specs=[row, state_spec(POOL_BUF), state_spec(SCONV_K - 1), state_spec(CCONV_K - 1)],
        out_shape=[jax.ShapeDtypeStruct(h.shape, f32),
                   jax.ShapeDtypeStruct((batch, POOL_BUF, W_BR), f32),
                   jax.ShapeDtypeStruct((batch, SCONV_K - 1, W_BR), f32),
                   jax.ShapeDtypeStruct((batch, CCONV_K - 1, W_BR), f32)],
        scratch_shapes=_mixer_scratch(lay),
        compiler_params=_params(ndim=2),
        name="mixer_prompt",
    )(h, *params)


def _mixer_sample_call(h, i, spool, ssc, scf, params, *, ns):
    l, nseq, _ = h.shape
    lay = _StepMajor(l, ns)
    row = pl.BlockSpec((l, ns, D_MODEL), lambda s: (0, s, 0))

    def state_in(nb):
        return pl.BlockSpec((None, nb, ns, W_BR), lambda s: (i, 0, s, 0))

    def state_out(nb):
        return pl.BlockSpec((nb, ns, W_BR), lambda s: (0, s, 0))

    nbs = (POOL_BUF, SCONV_K - 1, CCONV_K - 1)
    return pl.pallas_call(
        functools.partial(_mixer_sample_kernel, lay=lay),
        grid=(nseq // ns,),
        in_specs=[row] + [state_in(nb) for nb in nbs] + [_layer(a.shape, i) for a in params],
        out_specs=[row] + [state_out(nb) for nb in nbs],
        out_shape=[jax.ShapeDtypeStruct(h.shape, f32)] + [jax.ShapeDtypeStruct((nb, nseq, W_BR), f32) for nb in nbs],
        scratch_shapes=_mixer_scratch(lay),
        compiler_params=_params(),
        name="mixer_sample",
    )(h, spool, ssc, scf, *params)


def kernel(x_prompt, x_sample, p_prompt, p_sample, state_pool, state_sconv, state_cconv, g_f1_pre, w_f1_gate, w_f1_up, w_f1_down, g_f1_post, g_mix_pre, w_in, pool_w, pool_scale, w_pool_out, sc_conv, w_sc_out, cf_conv, cf_conv_b, cf_ln_g, cf_ln_b, w_cf_out, w_o, g_mix_post, g_f2_pre, w_f2_gate, w_f2_up, w_f2_down, g_f2_post, g_ple_pre, w_ple_gate, w_ple_proj, g_ple_post, g_final):
    depth = w_in.shape[0]
    batch, seq, _ = x_prompt.shape
    nseq, dec_seq, _ = x_sample.shape
    tm = 512
    ns = 32

    def vec(a):
        return a.reshape(depth, 1, a.shape[-1])

    def mat(a):
        return a.astype(bf16)

    f1 = (vec(g_f1_pre), mat(w_f1_gate), mat(w_f1_up), mat(w_f1_down), vec(g_f1_post))
    mix = (vec(g_mix_pre), mat(w_in), mat(pool_w), vec(pool_scale), mat(w_pool_out), sc_conv, mat(w_sc_out),
           cf_conv, vec(cf_conv_b), vec(cf_ln_g), vec(cf_ln_b), mat(w_cf_out), mat(w_o), vec(g_mix_post))
    f2 = (vec(g_f2_pre), mat(w_f2_gate), mat(w_f2_up), mat(w_f2_down), vec(g_f2_post),
          vec(g_ple_pre), mat(w_ple_gate), mat(w_ple_proj), vec(g_ple_post), g_final.reshape(1, -1))

    hp = x_prompt.reshape(batch * seq, D_MODEL)
    hs = x_sample.transpose(1, 0, 2).reshape(dec_seq * nseq, D_MODEL)
    pp = p_prompt.reshape(depth, batch * seq, PLE_DIM)
    ps = p_sample.transpose(0, 2, 1, 3).reshape(depth, dec_seq * nseq, PLE_DIM)
    spool, ssc, scf = (s.transpose(0, 2, 1, 3) for s in (state_pool, state_sconv, state_cconv))

    new_p, new_s = [], []
    for i in range(depth):
        hp = _ffn_call(hp, i, *f1, tm=tm)
        hs = _ffn_call(hs, i, *f1, tm=tm)
        hp, *st_p = _mixer_prompt_call(hp, i, mix, batch=batch, seq=seq, l=tm)
        hs, *st_s = _mixer_sample_call(hs.reshape(dec_seq, nseq, D_MODEL), i, spool, ssc, scf, mix, ns=ns)
        hs = hs.reshape(dec_seq * nseq, D_MODEL)
        new_p.append(st_p)
        new_s.append(st_s)
        final = i == depth - 1
        hp = _ffn_ple_call(hp, pp, i, *f2, tm=tm, final=final)
        hs = _ffn_ple_call(hs, ps, i, *f2, tm=tm, final=final)
    y_sample = hs.reshape(dec_seq, nseq, D_MODEL).transpose(1, 0, 2)
    stack_p = lambda k: jnp.stack([s[k] for s in new_p])
    stack_s = lambda k: jnp.stack([s[k] for s in new_s]).transpose(0, 2, 1, 3)
    return (hp.reshape(batch, seq, D_MODEL), y_sample,
            stack_p(0), stack_p(1), stack_p(2), stack_s(0), stack_s(1), stack_s(2))
```

```python
import functools

import jax
import jax.numpy as jnp
from jax import lax
from jax.experimental import pallas as pl
from jax.experimental.pallas import tpu as pltpu

D_MODEL = 1024
D_FF = 2816
PLE_DIM = 256
W_BR = 512
POOL_WINDOWS = (2, 4, 8, 16)
POOL_BUF = 15
SCONV_K = 3
CCONV_K = 31
PAST_LEN = 16384
EPS = 1e-6

LANES = 128
SUBLANES = 8
BF16_ROWS = 16
N_SLAB = W_BR // LANES
CONV_ROWS = 64

C_A, C_V, C_BG, C_CG, C_GA, C_GG, C_G0 = 0, 512, 1024, 1536, 2048, 2560, 3072
IN_COLS = C_G0 + 3 * D_MODEL

FF_CHUNKS = ((0, 512), (512, 512), (1024, 512), (1536, 512), (2048, 512), (2560, 256))
MIX_CHUNK = 512

VMEM_LIMIT = 56 * 1024 * 1024

f32 = jnp.float32
bf16 = jnp.bfloat16


def _round_up(x, m):
    return (x + m - 1) // m * m


def _rms(x, g):
    return x * lax.rsqrt(jnp.mean(x * x, axis=-1, keepdims=True) + EPS) * g


def _dot(a, b):
    return jnp.dot(a, b, preferred_element_type=f32)


def _resident(shape):
    nd = len(shape)
    return pl.BlockSpec(shape, lambda *_: (0,) * nd, pipeline_mode=pl.Buffered(1))


def _layer(shape, i):
    nd = len(shape)
    return pl.BlockSpec((None,) + tuple(shape[1:]), lambda *_: (i,) + (0,) * (nd - 1), pipeline_mode=pl.Buffered(1))


def _spec(a, i):
    return _resident(a.shape) if i is None else _layer(a.shape, i)


def _params(ndim=1):
    return pltpu.CompilerParams(dimension_semantics=("arbitrary",) * ndim, vmem_limit_bytes=VMEM_LIMIT)


def _cast_plan(src, i, nsteps, lin):
    _, r, c = src.shape
    div = 1
    while (r * div) % nsteps or (r * div // nsteps) % BF16_ROWS:
        div *= 2
    br = r * div // nsteps
    return (pl.BlockSpec((None, br, c), lambda *g: (i, lin(*g) // div, 0)),
            pl.BlockSpec((br, c), lambda *g: (lin(*g) // div, 0)),
            jax.ShapeDtypeStruct((r, c), bf16))


def _run_casts(srcs, dsts):
    for s, d in zip(srcs, dsts):
        d[...] = s[...].astype(bf16)


def _swiglu_into(acc_scr, u_scr, wg_ref, wu_ref, wd_ref):
    for n, (c0, cw) in enumerate(FF_CHUNKS):
        u = u_scr[...]
        g = _dot(u, wg_ref[:, c0:c0 + cw])
        up = _dot(u, wu_ref[:, c0:c0 + cw])
        a = (jax.nn.silu(g) * up).astype(bf16)
        d = _dot(a, wd_ref[c0:c0 + cw, :])
        if n == 0:
            acc_scr[...] = d
        else:
            acc_scr[...] += d


def _ffn_kernel(*refs, n_cast):
    h_ref, gpre_ref, wg_ref, wu_ref, wd_ref, gpost_ref = refs[:6]
    srcs, (o_ref, *dsts), (u_scr, acc_scr) = refs[6:6 + n_cast], refs[6 + n_cast:7 + 2 * n_cast], refs[7 + 2 * n_cast:]
    u_scr[...] = _rms(h_ref[...], gpre_ref[...]).astype(bf16)
    _swiglu_into(acc_scr, u_scr, wg_ref, wu_ref, wd_ref)
    o_ref[...] = h_ref[...] + 0.5 * _rms(acc_scr[...], gpost_ref[...])
    _run_casts(srcs, dsts)


def _ffn_ple_kernel(*refs, n_cast, final):
    (h_ref, p_ref, gpre_ref, wg_ref, wu_ref, wd_ref, gpost_ref,
     gppre_ref, wpg_ref, wpp_ref, gppost_ref, gfin_ref) = refs[:12]
    srcs, (o_ref, *dsts), (u_scr, acc_scr) = (refs[12:12 + n_cast], refs[12 + n_cast:13 + 2 * n_cast],
                                             refs[13 + 2 * n_cast:])
    u_scr[...] = _rms(h_ref[...], gpre_ref[...]).astype(bf16)
    _swiglu_into(acc_scr, u_scr, wg_ref, wu_ref, wd_ref)
    h = h_ref[...] + 0.5 * _rms(acc_scr[...], gpost_ref[...])
    gate = jax.nn.sigmoid(_dot(_rms(h, gppre_ref[...]).astype(bf16), wpg_ref[...]))
    proj = _dot(p_ref[...].astype(bf16), wpp_ref[...])
    h = h + _rms(proj * gate, gppost_ref[...])
    if final:
        h = _rms(h, gfin_ref[...])
    o_ref[...] = h
    _run_casts(srcs, dsts)


def _token_call(kern, name, h, extra, extra_specs, params, casts, *, tm):
    steps = h.shape[0] // tm
    row = pl.BlockSpec((tm, D_MODEL), lambda r: (r, 0))
    plans = [_cast_plan(src, i, steps, lambda r: r) for src, i in casts]
    return pl.pallas_call(
        functools.partial(kern, n_cast=len(casts)),
        grid=(steps,),
        in_specs=[row] + extra_specs + [_spec(a, i) for a, i in params] + [p[0] for p in plans],
        out_specs=[row] + [p[1] for p in plans],
        out_shape=[jax.ShapeDtypeStruct(h.shape, f32)] + [p[2] for p in plans],
        scratch_shapes=[pltpu.VMEM((tm, D_MODEL), bf16), pltpu.VMEM((tm, D_MODEL), f32)],
        compiler_params=_params(),
        name=name,
    )(h, *extra, *[a for a, _ in params], *[src for src, _ in casts])


def _ffn_call(h, params, casts=(), *, tm):
    return _token_call(_ffn_kernel, "ffn", h, [], [], params, casts, tm=tm)


def _ffn_ple_call(h, p, i, params, casts=(), *, tm, final):
    prow = pl.BlockSpec((None, tm, PLE_DIM), lambda r: (i, r, 0))
    return _token_call(functools.partial(_ffn_ple_kernel, final=final), "ffn_ple", h, [p], [prow], params, casts, tm=tm)


OFF_POOL = _round_up(POOL_BUF, SUBLANES)
OFF_SC = _round_up(SCONV_K - 1, SUBLANES)
OFF_CF = _round_up(CCONV_K - 1, SUBLANES)


class _OneSequence:
    def __init__(self, l):
        self.l, self.tm = l, l
        self.blocks = tuple(range(0, l, CONV_ROWS))

    def scratch(self, cols, dtype):
        return pltpu.VMEM((self.l, cols), dtype)

    def slab(self, off):
        return pltpu.VMEM((N_SLAB, off + self.l, LANES), f32)

    def load(self, ref, cs=slice(None)):
        return ref[:, cs]

    def store(self, ref, cs, val):
        ref[:, cs] = val

    def blk_load(self, ref, b, cs):
        return ref[b:b + CONV_ROWS, cs]

    def blk_store(self, ref, b, cs, val):
        ref[b:b + CONV_ROWS, cs] = val

    def win(self, xp, j, start, b):
        return xp[j, start + b:start + b + CONV_ROWS, :]

    def tap(self, w_ref, k, cs):
        return w_ref[k:k + 1, cs]

    def to_slab(self, xp, off, j, val):
        xp[j, off:off + self.l, :] = val

    def blk2d(self, val):
        return val

    def blk_like(self, val2d):
        return val2d

    def row_index(self, b):
        return b + lax.broadcasted_iota(jnp.int32, (CONV_ROWS, LANES), 0)


class _StepMajor:
    def __init__(self, l, ns):
        self.l, self.ns, self.tm = l, ns, l * ns
        self.blocks = tuple(range(0, ns, BF16_ROWS))

    def scratch(self, cols, dtype):
        return pltpu.VMEM((self.l, self.ns, cols), dtype)

    def slab(self, off):
        return pltpu.VMEM((N_SLAB, off + self.l, self.ns, LANES), f32)

    def load(self, ref, cs=slice(None)):
        v = ref[:, :, cs]
        return v.reshape(self.tm, v.shape[-1])

    def store(self, ref, cs, val):
        ref[:, :, cs] = val.reshape(self.l, self.ns, val.shape[-1])

    def blk_load(self, ref, b, cs):
        return ref[:, b:b + BF16_ROWS, cs]

    def blk_store(self, ref, b, cs, val):
        ref[:, b:b + BF16_ROWS, cs] = val

    def win(self, xp, j, start, b):
        return xp[j, start:start + self.l, b:b + BF16_ROWS, :]

    def tap(self, w_ref, k, cs):
        return w_ref[k:k + 1, cs].reshape(1, 1, LANES)

    def to_slab(self, xp, off, j, val):
        xp[j, off:off + self.l, :, :] = val.reshape(self.l, self.ns, LANES)

    def blk2d(self, val):
        return val.reshape(self.l * BF16_ROWS, LANES)

    def blk_like(self, val2d):
        return val2d.reshape(self.l, BF16_ROWS, LANES)


def _mixer_body(lay, read_h, write_o, pos0,
                gpre_ref, win_ref, poolw_ref, pscale_ref, wpo_ref, scw_ref, wso_ref, cfw_ref, cfb_ref,
                lng_ref, lnb_ref, wco_ref, wo_ref, gpost_ref,
                u_scr, zb_scr, g_scr, y_scr, cc_scr, xp_pool, xp_sc, xp_cf):
    def _project_pool_shortconv():
        u_scr[...] = _rms(read_h(), gpre_ref[...]).astype(bf16)
        zc = _dot(u_scr[...], win_ref[:, C_GA:C_GA + 2 * W_BR])
        for j in range(N_SLAB):
            ga = zc[:, j * LANES:(j + 1) * LANES]
            gg = zc[:, W_BR + j * LANES:W_BR + (j + 1) * LANES]
            lay.to_slab(xp_cf, OFF_CF, j, ga * jax.nn.sigmoid(gg))
        lay.store(zb_scr, slice(None), _dot(u_scr[...], win_ref[:, C_V:C_V + 3 * W_BR]))
        for j in range(N_SLAB):
            v = lay.load(zb_scr, slice(j * LANES, (j + 1) * LANES))
            cg = lay.load(zb_scr, slice(2 * W_BR + j * LANES, 2 * W_BR + (j + 1) * LANES))
            lay.to_slab(xp_sc, OFF_SC, j, cg * v)
        za = _dot(u_scr[...], win_ref[:, C_A:C_A + W_BR])
        for j in range(N_SLAB):
            lay.to_slab(xp_pool, OFF_POOL, j, za[:, j * LANES:(j + 1) * LANES])

        for j in range(N_SLAB):
            cs = slice(j * LANES, (j + 1) * LANES)
            taps = [lay.tap(scw_ref, k, cs) for k in range(SCONV_K)]
            for b in lay.blocks:
                cv = None
                for k in range(SCONV_K):
                    term = lay.win(xp_sc, j, OFF_SC - (SCONV_K - 1) + k, b) * taps[k]
                    cv = term if cv is None else cv + term
                bg = lay.blk_load(zb_scr, b, slice(W_BR + j * LANES, W_BR + (j + 1) * LANES))
                lay.blk_store(y_scr, b, slice(W_BR + j * LANES, W_BR + (j + 1) * LANES), (bg * cv).astype(bf16))

        for j, w in enumerate(POOL_WINDOWS):
            cs = slice(j * LANES, (j + 1) * LANES)
            for b in lay.blocks:
                cur = lay.win(xp_pool, j, OFF_POOL, b)
                s = cur
                for i in range(1, w):
                    s = s + lay.win(xp_pool, j, OFF_POOL - i, b)
                if pos0 is not None and b < w - 1:
                    mean = s / jnp.minimum(w, pos0 + lay.row_index(b) + 1).astype(f32)
                else:
                    mean = s * (1.0 / w)
                mm = lay.blk2d(mean - cur).astype(bf16)
                ya = _dot(mm, poolw_ref[j]) * pscale_ref[:, cs]
                lay.blk_store(y_scr, b, cs, lay.blk_like(ya.astype(bf16)))

    def _gates_and_conformer_conv():
        for b in range(3):
            for c0 in range(0, D_MODEL, MIX_CHUNK):
                cs = slice(b * D_MODEL + c0, b * D_MODEL + c0 + MIX_CHUNK)
                g_scr[:, cs] = jax.nn.sigmoid(_dot(u_scr[...], win_ref[:, C_G0 + cs.start:C_G0 + cs.stop]))
        for j in range(N_SLAB):
            cs = slice(j * LANES, (j + 1) * LANES)
            taps = [lay.tap(cfw_ref, k, cs) for k in range(CCONV_K)]
            bias = cfb_ref[:, cs]
            for b in lay.blocks:
                cc = None
                for k in range(CCONV_K):
                    term = lay.win(xp_cf, j, OFF_CF - (CCONV_K - 1) + k, b) * taps[k]
                    cc = term if cc is None else cc + term
                lay.blk_store(cc_scr, b, cs, cc + bias)
        x = lay.load(cc_scr)
        xc = x - jnp.mean(x, axis=-1, keepdims=True)
        ln = xc * lax.rsqrt(jnp.mean(xc * xc, axis=-1, keepdims=True) + EPS) * lng_ref[...] + lnb_ref[...]
        lay.store(y_scr, slice(2 * W_BR, 3 * W_BR), jax.nn.silu(ln).astype(bf16))

    def _combine_and_project():
        m = []
        for c0 in range(0, D_MODEL, MIX_CHUNK):
            cs = slice(c0, c0 + MIX_CHUNK)
            acc = None
            for b, w_ref in enumerate((wpo_ref, wso_ref, wco_ref)):
                yb = lay.load(y_scr, slice(b * W_BR, (b + 1) * W_BR))
                term = g_scr[:, b * D_MODEL + c0:b * D_MODEL + c0 + MIX_CHUNK] * _dot(yb, w_ref[:, cs])
                acc = term if acc is None else acc + term
            m.append(acc.astype(bf16))
        mo = _dot(jnp.concatenate(m, axis=-1), wo_ref[...])
        write_o(read_h() + _rms(mo, gpost_ref[...]))

    _project_pool_shortconv()
    _gates_and_conformer_conv()
    _combine_and_project()


N_MIX_PARAMS = 14


def _mixer_scratch(lay):
    tm = lay.tm
    return [
        pltpu.VMEM((tm, D_MODEL), bf16),
        lay.scratch(3 * W_BR, f32),
        pltpu.VMEM((tm, 3 * D_MODEL), f32),
        lay.scratch(3 * W_BR, bf16),
        lay.scratch(W_BR, f32),
        lay.slab(OFF_POOL), lay.slab(OFF_SC), lay.slab(OFF_CF),
    ]


def _mixer_prompt_kernel(h_ref, *refs, lay, n_cast):
    params = refs[:N_MIX_PARAMS]
    srcs = refs[N_MIX_PARAMS:N_MIX_PARAMS + n_cast]
    o_ref, npool_ref, nsc_ref, ncf_ref = refs[N_MIX_PARAMS + n_cast:N_MIX_PARAMS + n_cast + 4]
    dsts = refs[N_MIX_PARAMS + n_cast + 4:N_MIX_PARAMS + 2 * n_cast + 4]
    scr = refs[N_MIX_PARAMS + 2 * n_cast + 4:]
    xp_pool, xp_sc, xp_cf = scr[-3:]
    l = lay.l
    c = pl.program_id(1)
    carried = ((xp_pool, OFF_POOL, POOL_BUF, npool_ref), (xp_sc, OFF_SC, SCONV_K - 1, nsc_ref),
               (xp_cf, OFF_CF, CCONV_K - 1, ncf_ref))

    @pl.when(c == 0)
    def _():
        for xp, off, _, _ in carried:
            xp[:, 0:off, :] = jnp.zeros((N_SLAB, off, LANES), f32)

    def write_o(val):
        o_ref[...] = val

    _mixer_body(lay, lambda: h_ref[...], write_o, c * l, *params, *scr)

    for xp, off, nb, _ in carried:
        xp[:, off - nb:off, :] = xp[:, off + l - nb:off + l, :]

    @pl.when(c == pl.num_programs(1) - 1)
    def _():
        for xp, off, nb, out_ref in carried:
            for j in range(N_SLAB):
                out_ref[0, :, j * LANES:(j + 1) * LANES] = xp[j, off - nb:off, :]

    _run_casts(srcs, dsts)


def _mixer_sample_kernel(h_ref, spool_ref, ssc_ref, scf_ref, *refs, lay):
    params = refs[:N_MIX_PARAMS]
    o_ref, npool_ref, nsc_ref, ncf_ref = refs[N_MIX_PARAMS:N_MIX_PARAMS + 4]
    scr = refs[N_MIX_PARAMS + 4:]
    xp_pool, xp_sc, xp_cf = scr[-3:]
    l = lay.l
    carried = ((xp_pool, OFF_POOL, POOL_BUF, spool_ref, npool_ref), (xp_sc, OFF_SC, SCONV_K - 1, ssc_ref, nsc_ref),
               (xp_cf, OFF_CF, CCONV_K - 1, scf_ref, ncf_ref))
    for xp, off, nb, in_ref, _ in carried:
        for j in range(N_SLAB):
            xp[j, off - nb:off, :, :] = in_ref[:, :, j * LANES:(j + 1) * LANES]
    assert PAST_LEN + 1 >= max(POOL_WINDOWS)

    def write_o(val):
        o_ref[...] = val.reshape(l, lay.ns, D_MODEL)

    _mixer_body(lay, lambda: h_ref[...].reshape(lay.tm, D_MODEL), write_o, None, *params, *scr)
    for xp, off, nb, _, out_ref in carried:
        for j in range(N_SLAB):
            out_ref[:, :, j * LANES:(j + 1) * LANES] = xp[j, off + l - nb:off + l, :, :]


def _mixer_prompt_call(h, params, casts, *, batch, seq, l):
    ch = seq // l
    lay = _OneSequence(l)
    row = pl.BlockSpec((l, D_MODEL), lambda b, c: (b * ch + c, 0))
    plans = [_cast_plan(src, i, batch * ch, lambda b, c: b * ch + c) for src, i in casts]

    def state_spec(nb):
        return pl.BlockSpec((1, nb, W_BR), lambda b, c: (b, 0, 0))

    return pl.pallas_call(
        functools.partial(_mixer_prompt_kernel, lay=lay, n_cast=len(casts)),
        grid=(batch, ch),
        in_specs=[row] + [_spec(a, i) for a, i in params] + [p[0] for p in plans],
        out_specs=[row, state_spec(POOL_BUF), state_spec(SCONV_K - 1), state_spec(CCONV_K - 1)] + [p[1] for p in plans],
        out_shape=[jax.ShapeDtypeStruct(h.shape, f32),
                   jax.ShapeDtypeStruct((batch, POOL_BUF, W_BR), f32),
                   jax.ShapeDtypeStruct((batch, SCONV_K - 1, W_BR), f32),
                   jax.ShapeDtypeStruct((batch, CCONV_K - 1, W_BR), f32)] + [p[2] for p in plans],
        scratch_shapes=_mixer_scratch(lay),
        compiler_params=_params(ndim=2),
        name="mixer_prompt",
    )(h, *[a for a, _ in params], *[src for src, _ in casts])


def _mixer_sample_call(h, i, spool, ssc, scf, params, *, ns):
    l, nseq, _ = h.shape
    lay = _StepMajor(l, ns)
    row = pl.BlockSpec((l, ns, D_MODEL), lambda s: (0, s, 0))

    def state_in(nb):
        return pl.BlockSpec((None, nb, ns, W_BR), lambda s: (i, 0, s, 0))

    def state_out(nb):
        return pl.BlockSpec((nb, ns, W_BR), lambda s: (0, s, 0))

    nbs = (POOL_BUF, SCONV_K - 1, CCONV_K - 1)
    return pl.pallas_call(
        functools.partial(_mixer_sample_kernel, lay=lay),
        grid=(nseq // ns,),
        in_specs=[row] + [state_in(nb) for nb in nbs] + [_spec(a, j) for a, j in params],
        out_specs=[row] + [state_out(nb) for nb in nbs],
        out_shape=[jax.ShapeDtypeStruct(h.shape, f32)] + [jax.ShapeDtypeStruct((nb, nseq, W_BR), f32) for nb in nbs],
        scratch_shapes=_mixer_scratch(lay),
        compiler_params=_params(),
        name="mixer_sample",
    )(h, spool, ssc, scf, *[a for a, _ in params])


def kernel(x_prompt, x_sample, p_prompt, p_sample, state_pool, state_sconv, state_cconv, g_f1_pre, w_f1_gate, w_f1_up, w_f1_down, g_f1_post, g_mix_pre, w_in, pool_w, pool_scale, w_pool_out, sc_conv, w_sc_out, cf_conv, cf_conv_b, cf_ln_g, cf_ln_b, w_cf_out, w_o, g_mix_post, g_f2_pre, w_f2_gate, w_f2_up, w_f2_down, g_f2_post, g_ple_pre, w_ple_gate, w_ple_proj, g_ple_post, g_final):
    depth = w_in.shape[0]
    batch, seq, _ = x_prompt.shape
    nseq, dec_seq, _ = x_sample.shape
    tm = 512
    ns = 32

    def vec(a, i):
        return (a.reshape(depth, 1, a.shape[-1]), i)

    def whole(a):
        return (a, None)

    gfin = whole(g_final.reshape(1, -1))
    pool_w2 = pool_w.reshape(depth, W_BR, LANES)

    hp = x_prompt.reshape(batch * seq, D_MODEL)
    hs = x_sample.transpose(1, 0, 2).reshape(dec_seq * nseq, D_MODEL)
    pp = p_prompt.reshape(depth, batch * seq, PLE_DIM)
    ps = p_sample.transpose(0, 2, 1, 3).reshape(depth, dec_seq * nseq, PLE_DIM)
    spool, ssc, scf = (s.transpose(0, 2, 1, 3) for s in (state_pool, state_sconv, state_cconv))

    f1_w = [w[0].astype(bf16) for w in (w_f1_gate, w_f1_up, w_f1_down)]
    new_p, new_s = [], []
    for i in range(depth):
        f1 = [vec(g_f1_pre, i)] + [whole(w) for w in f1_w] + [vec(g_f1_post, i)]
        mix_src = (w_in, pool_w2, w_pool_out, w_sc_out, w_cf_out, w_o)
        hp, *mix_w = _ffn_call(hp, f1, [(w, i) for w in mix_src], tm=tm)
        hs, = _ffn_call(hs, f1, tm=tm)
        wi, pw, wpo, wso, wco, wo = mix_w
        mix = [vec(g_mix_pre, i), whole(wi), whole(pw.reshape(len(POOL_WINDOWS), LANES, LANES)), vec(pool_scale, i),
               whole(wpo), (sc_conv, i), whole(wso), (cf_conv, i), vec(cf_conv_b, i), vec(cf_ln_g, i),
               vec(cf_ln_b, i), whole(wco), whole(wo), vec(g_mix_post, i)]
        f2_src = (w_f2_gate, w_f2_up, w_f2_down, w_ple_gate, w_ple_proj)
        hp, *rest = _mixer_prompt_call(hp, mix, [(w, i) for w in f2_src], batch=batch, seq=seq, l=tm)
        st_p, f2_w = rest[:3], rest[3:]
        hs, *st_s = _mixer_sample_call(hs.reshape(dec_seq, nseq, D_MODEL), i, spool, ssc, scf, mix, ns=ns)
        hs = hs.reshape(dec_seq * nseq, D_MODEL)
        new_p.append(st_p)
        new_s.append(st_s)
        wg, wu, wd, wpg, wpp = f2_w
        f2 = [vec(g_f2_pre, i), whole(wg), whole(wu), whole(wd), vec(g_f2_post, i),
              vec(g_ple_pre, i), whole(wpg), whole(wpp), vec(g_ple_post, i), gfin]
        final = i == depth - 1
        nxt = [] if final else [(w, i + 1) for w in (w_f1_gate, w_f1_up, w_f1_down)]
        hp, *f1_w = _ffn_ple_call(hp, pp, i, f2, nxt, tm=tm, final=final)
        hs, = _ffn_ple_call(hs, ps, i, f2, tm=tm, final=final)
    y_sample = hs.reshape(dec_seq, nseq, D_MODEL).transpose(1, 0, 2)
    stack_p = lambda k: jnp.stack([s[k] for s in new_p])
    stack_s = lambda k: jnp.stack([s[k] for s in new_s]).transpose(0, 2, 1, 3)
    return (hp.reshape(batch, seq, D_MODEL), y_sample,
            stack_p(0), stack_p(1), stack_p(2), stack_s(0), stack_s(1), stack_s(2))
```

```python
import functools

import jax
import jax.numpy as jnp
from jax import lax
from jax.experimental import pallas as pl
from jax.experimental.pallas import tpu as pltpu

D_MODEL = 1024
D_FF = 2816
PLE_DIM = 256
W_BR = 512
POOL_WINDOWS = (2, 4, 8, 16)
POOL_BUF = 15
SCONV_K = 3
CCONV_K = 31
PAST_LEN = 16384
EPS = 1e-6

LANES = 128
SUBLANES = 8
BF16_ROWS = 16
N_SLAB = W_BR // LANES
CONV_ROWS = 64

C_A, C_V, C_BG, C_CG, C_GA, C_GG, C_G0 = 0, 512, 1024, 1536, 2048, 2560, 3072
IN_COLS = C_G0 + 3 * D_MODEL

FF_CHUNKS = ((0, 1024), (1024, 1024), (2048, 768))
MIX_CHUNK = 512

VMEM_LIMIT = 56 * 1024 * 1024

f32 = jnp.float32
bf16 = jnp.bfloat16


def _round_up(x, m):
    return (x + m - 1) // m * m


def _rms(x, g):
    return x * lax.rsqrt(jnp.mean(x * x, axis=-1, keepdims=True) + EPS) * g


def _dot(a, b):
    return jnp.dot(a, b, preferred_element_type=f32)


def _resident(shape):
    nd = len(shape)
    return pl.BlockSpec(shape, lambda *_: (0,) * nd, pipeline_mode=pl.Buffered(1))


def _layer(shape, i):
    nd = len(shape)
    return pl.BlockSpec((None,) + tuple(shape[1:]), lambda *_: (i,) + (0,) * (nd - 1), pipeline_mode=pl.Buffered(1))


def _spec(a, i):
    return _resident(a.shape) if i is None else _layer(a.shape, i)


def _params(ndim=1):
    return pltpu.CompilerParams(dimension_semantics=("arbitrary",) * ndim, vmem_limit_bytes=VMEM_LIMIT)


def _cast_plan(casts, nsteps, lin):
    ins, gains, outs, shapes, arrays, gain_arrays, scaled = [], [], [], [], [], [], []
    for k, (src, i, gain) in enumerate(casts):
        _, r, c = src.shape
        div = 1
        while (r * div) % nsteps or (r * div // nsteps) % BF16_ROWS:
            div *= 2
        br = r * div // nsteps
        ins.append(pl.BlockSpec((None, br, c), lambda *g, i=i, div=div: (i, lin(*g) // div, 0)))
        outs.append(pl.BlockSpec((br, c), lambda *g, div=div: (lin(*g) // div, 0)))
        shapes.append(jax.ShapeDtypeStruct((r, c), bf16))
        arrays.append(src)
        if gain is not None:
            gains.append(pl.BlockSpec((None, br, 1), lambda *g, i=i, div=div: (i, lin(*g) // div, 0)))
            gain_arrays.append(gain.reshape(gain.shape[0], r, 1))
            scaled.append(k)
    return ins + gains, outs, shapes, arrays + gain_arrays, tuple(scaled)


def _run_casts(srcs, dsts, scaled):
    gains = dict(zip(scaled, srcs[len(dsts):]))
    for k, d in enumerate(dsts):
        w = srcs[k][...]
        if k in gains:
            w = w * gains[k][...]
        d[...] = w.astype(bf16)


def _row_scale(x):
    return lax.rsqrt(jnp.mean(x * x, axis=-1, keepdims=True) + EPS)


def _swiglu_into(acc_scr, u_scr, rs, wg_ref, wu_ref, wd_ref):
    for n, (c0, cw) in enumerate(FF_CHUNKS):
        u = u_scr[...]
        g = _dot(u, wg_ref[:, c0:c0 + cw]) * rs
        up = _dot(u, wu_ref[:, c0:c0 + cw]) * rs
        a = (jax.nn.silu(g) * up).astype(bf16)
        d = _dot(a, wd_ref[c0:c0 + cw, :])
        if n == 0:
            acc_scr[...] = d
        else:
            acc_scr[...] += d


def _split_refs(refs, n_in, n_out, scaled):
    n_cast_in = n_out + len(scaled)
    return (refs[:n_in], refs[n_in:n_in + n_cast_in], refs[n_in + n_cast_in],
            refs[n_in + n_cast_in + 1:n_in + n_cast_in + 1 + n_out], refs[n_in + n_cast_in + 1 + n_out:])


def _ffn_kernel(*refs, n_cast, scaled):
    (h_ref, wg_ref, wu_ref, wd_ref, gpost_ref), srcs, o_ref, dsts, (u_scr, acc_scr) = _split_refs(refs, 5, n_cast, scaled)
    h = h_ref[...]
    u_scr[...] = h.astype(bf16)
    _swiglu_into(acc_scr, u_scr, _row_scale(h), wg_ref, wu_ref, wd_ref)
    o_ref[...] = h_ref[...] + _rms(acc_scr[...], 0.5 * gpost_ref[...])
    _run_casts(srcs, dsts, scaled)


def _ffn_ple_kernel(*refs, n_cast, scaled, final):
    ((h_ref, p_ref, wg_ref, wu_ref, wd_ref, gpost_ref, wpg_ref, wpp_ref, gppost_ref, gfin_ref),
     srcs, o_ref, dsts, (u_scr, acc_scr)) = _split_refs(refs, 10, n_cast, scaled)
    h = h_ref[...]
    u_scr[...] = h.astype(bf16)
    _swiglu_into(acc_scr, u_scr, _row_scale(h), wg_ref, wu_ref, wd_ref)
    h = h_ref[...] + _rms(acc_scr[...], 0.5 * gpost_ref[...])
    gate = jax.nn.sigmoid(_dot(h.astype(bf16), wpg_ref[...]) * _row_scale(h))
    proj = _dot(p_ref[...].astype(bf16), wpp_ref[...])
    h = h + _rms(proj * gate, gppost_ref[...])
    if final:
        h = _rms(h, gfin_ref[...])
    o_ref[...] = h
    _run_casts(srcs, dsts, scaled)


def _token_call(kern, name, h, extra, extra_specs, params, casts, *, tm):
    steps = h.shape[0] // tm
    row = pl.BlockSpec((tm, D_MODEL), lambda r: (r, 0))
    c_in, c_out, c_shapes, c_arrays, scaled = _cast_plan(casts, steps, lambda r: r)
    return pl.pallas_call(
        functools.partial(kern, n_cast=len(casts), scaled=scaled),
        grid=(steps,),
        in_specs=[row] + extra_specs + [_spec(a, i) for a, i in params] + c_in,
        out_specs=[row] + c_out,
        out_shape=[jax.ShapeDtypeStruct(h.shape, f32)] + c_shapes,
        scratch_shapes=[pltpu.VMEM((tm, D_MODEL), bf16), pltpu.VMEM((tm, D_MODEL), f32)],
        compiler_params=_params(),
        name=name,
    )(h, *extra, *[a for a, _ in params], *c_arrays)


def _ffn_call(h, params, casts=(), *, tm):
    return _token_call(_ffn_kernel, "ffn", h, [], [], params, casts, tm=tm)


def _ffn_ple_call(h, p, i, params, casts=(), *, tm, final):
    prow = pl.BlockSpec((None, tm, PLE_DIM), lambda r: (i, r, 0))
    return _token_call(functools.partial(_ffn_ple_kernel, final=final), "ffn_ple", h, [p], [prow], params, casts, tm=tm)


OFF_POOL = _round_up(POOL_BUF, SUBLANES)
OFF_SC = _round_up(SCONV_K - 1, SUBLANES)
OFF_CF = _round_up(CCONV_K - 1, SUBLANES)


class _OneSequence:
    def __init__(self, l):
        self.l, self.tm = l, l
        self.blocks = tuple(range(0, l, CONV_ROWS))

    def scratch(self, cols, dtype):
        return pltpu.VMEM((self.l, cols), dtype)

    def slab(self, off):
        return pltpu.VMEM((N_SLAB, off + self.l, LANES), f32)

    def load(self, ref, cs=slice(None)):
        return ref[:, cs]

    def store(self, ref, cs, val):
        ref[:, cs] = val

    def blk_load(self, ref, b, cs):
        return ref[b:b + CONV_ROWS, cs]

    def blk_store(self, ref, b, cs, val):
        ref[b:b + CONV_ROWS, cs] = val

    def win(self, xp, j, start, b):
        return xp[j, start + b:start + b + CONV_ROWS, :]

    def tap(self, w_ref, k, cs):
        return w_ref[k:k + 1, cs]

    def to_slab(self, xp, off, j, val):
        xp[j, off:off + self.l, :] = val

    def blk2d(self, val):
        return val

    def blk_like(self, val2d):
        return val2d

    def row_index(self, b):
        return b + lax.broadcasted_iota(jnp.int32, (CONV_ROWS, LANES), 0)


class _StepMajor:
    def __init__(self, l, ns):
        self.l, self.ns, self.tm = l, ns, l * ns
        self.blocks = tuple(range(0, ns, BF16_ROWS))

    def scratch(self, cols, dtype):
        return pltpu.VMEM((self.l, self.ns, cols), dtype)

    def slab(self, off):
        return pltpu.VMEM((N_SLAB, off + self.l, self.ns, LANES), f32)

    def load(self, ref, cs=slice(None)):
        v = ref[:, :, cs]
        return v.reshape(self.tm, v.shape[-1])

    def store(self, ref, cs, val):
        ref[:, :, cs] = val.reshape(self.l, self.ns, val.shape[-1])

    def blk_load(self, ref, b, cs):
        return ref[:, b:b + BF16_ROWS, cs]

    def blk_store(self, ref, b, cs, val):
        ref[:, b:b + BF16_ROWS, cs] = val

    def win(self, xp, j, start, b):
        return xp[j, start:start + self.l, b:b + BF16_ROWS, :]

    def tap(self, w_ref, k, cs):
        return w_ref[k:k + 1, cs].reshape(1, 1, LANES)

    def to_slab(self, xp, off, j, val):
        xp[j, off:off + self.l, :, :] = val.reshape(self.l, self.ns, LANES)

    def blk2d(self, val):
        return val.reshape(self.l * BF16_ROWS, LANES)

    def blk_like(self, val2d):
        return val2d.reshape(self.l, BF16_ROWS, LANES)


def _mixer_body(lay, read_h, write_o, pos0,
                gpre_ref, win_ref, poolw_ref, pscale_ref, wpo_ref, scw_ref, wso_ref, cfw_ref, cfb_ref,
                lng_ref, lnb_ref, wco_ref, wo_ref, gpost_ref,
                u_scr, zb_scr, g_scr, y_scr, cc_scr, xp_pool, xp_sc, xp_cf):
    def _project_pool_shortconv():
        u_scr[...] = _rms(read_h(), gpre_ref[...]).astype(bf16)
        zc = _dot(u_scr[...], win_ref[:, C_GA:C_GA + 2 * W_BR])
        for j in range(N_SLAB):
            ga = zc[:, j * LANES:(j + 1) * LANES]
            gg = zc[:, W_BR + j * LANES:W_BR + (j + 1) * LANES]
            lay.to_slab(xp_cf, OFF_CF, j, ga * jax.nn.sigmoid(gg))
        lay.store(zb_scr, slice(None), _dot(u_scr[...], win_ref[:, C_V:C_V + 3 * W_BR]))
        for j in range(N_SLAB):
            v = lay.load(zb_scr, slice(j * LANES, (j + 1) * LANES))
            cg = lay.load(zb_scr, slice(2 * W_BR + j * LANES, 2 * W_BR + (j + 1) * LANES))
            lay.to_slab(xp_sc, OFF_SC, j, cg * v)
        za = _dot(u_scr[...], win_ref[:, C_A:C_A + W_BR])
        for j in range(N_SLAB):
            lay.to_slab(xp_pool, OFF_POOL, j, za[:, j * LANES:(j + 1) * LANES])

        for j in range(N_SLAB):
            cs = slice(j * LANES, (j + 1) * LANES)
            taps = [lay.tap(scw_ref, k, cs) for k in range(SCONV_K)]
            for b in lay.blocks:
                cv = None
                for k in range(SCONV_K):
                    term = lay.win(xp_sc, j, OFF_SC - (SCONV_K - 1) + k, b) * taps[k]
                    cv = term if cv is None else cv + term
                bg = lay.blk_load(zb_scr, b, slice(W_BR + j * LANES, W_BR + (j + 1) * LANES))
                lay.blk_store(y_scr, b, slice(W_BR + j * LANES, W_BR + (j + 1) * LANES), (bg * cv).astype(bf16))

        for j, w in enumerate(POOL_WINDOWS):
            cs = slice(j * LANES, (j + 1) * LANES)
            for b in lay.blocks:
                cur = lay.win(xp_pool, j, OFF_POOL, b)
                s = cur
                for i in range(1, w):
                    s = s + lay.win(xp_pool, j, OFF_POOL - i, b)
                if pos0 is not None and b < w - 1:
                    mean = s / jnp.minimum(w, pos0 + lay.row_index(b) + 1).astype(f32)
                else:
                    mean = s * (1.0 / w)
                mm = lay.blk2d(mean - cur).astype(bf16)
                ya = _dot(mm, poolw_ref[j]) * pscale_ref[:, cs]
                lay.blk_store(y_scr, b, cs, lay.blk_like(ya.astype(bf16)))

    def _gates_and_conformer_conv():
        for b in range(3):
            for c0 in range(0, D_MODEL, MIX_CHUNK):
                cs = slice(b * D_MODEL + c0, b * D_MODEL + c0 + MIX_CHUNK)
                g_scr[:, cs] = jax.nn.sigmoid(_dot(u_scr[...], win_ref[:, C_G0 + cs.start:C_G0 + cs.stop]))
        for j in range(N_SLAB):
            cs = slice(j * LANES, (j + 1) * LANES)
            taps = [lay.tap(cfw_ref, k, cs) for k in range(CCONV_K)]
            bias = cfb_ref[:, cs]
            for b in lay.blocks:
                cc = None
                for k in range(CCONV_K):
                    term = lay.win(xp_cf, j, OFF_CF - (CCONV_K - 1) + k, b) * taps[k]
                    cc = term if cc is None else cc + term
                lay.blk_store(cc_scr, b, cs, cc + bias)
        x = lay.load(cc_scr)
        xc = x - jnp.mean(x, axis=-1, keepdims=True)
        ln = xc * lax.rsqrt(jnp.mean(xc * xc, axis=-1, keepdims=True) + EPS) * lng_ref[...] + lnb_ref[...]
        lay.store(y_scr, slice(2 * W_BR, 3 * W_BR), jax.nn.silu(ln).astype(bf16))

    def _combine_and_project():
        m = []
        for c0 in range(0, D_MODEL, MIX_CHUNK):
            cs = slice(c0, c0 + MIX_CHUNK)
            acc = None
            for b, w_ref in enumerate((wpo_ref, wso_ref, wco_ref)):
                yb = lay.load(y_scr, slice(b * W_BR, (b + 1) * W_BR))
                term = g_scr[:, b * D_MODEL + c0:b * D_MODEL + c0 + MIX_CHUNK] * _dot(yb, w_ref[:, cs])
                acc = term if acc is None else acc + term
            m.append(acc.astype(bf16))
        mo = _dot(jnp.concatenate(m, axis=-1), wo_ref[...])
        write_o(read_h() + _rms(mo, gpost_ref[...]))

    _project_pool_shortconv()
    _gates_and_conformer_conv()
    _combine_and_project()


N_MIX_PARAMS = 14


def _mixer_scratch(lay):
    tm = lay.tm
    return [
        pltpu.VMEM((tm, D_MODEL), bf16),
        lay.scratch(3 * W_BR, f32),
        pltpu.VMEM((tm, 3 * D_MODEL), f32),
        lay.scratch(3 * W_BR, bf16),
        lay.scratch(W_BR, f32),
        lay.slab(OFF_POOL), lay.slab(OFF_SC), lay.slab(OFF_CF),
    ]


def _mixer_prompt_kernel(h_ref, *refs, lay, n_cast, scaled):
    params = refs[:N_MIX_PARAMS]
    n_src = n_cast + len(scaled)
    srcs = refs[N_MIX_PARAMS:N_MIX_PARAMS + n_src]
    o_ref, npool_ref, nsc_ref, ncf_ref = refs[N_MIX_PARAMS + n_src:N_MIX_PARAMS + n_src + 4]
    dsts = refs[N_MIX_PARAMS + n_src + 4:N_MIX_PARAMS + n_src + n_cast + 4]
    scr = refs[N_MIX_PARAMS + n_src + n_cast + 4:]
    xp_pool, xp_sc, xp_cf = scr[-3:]
    l = lay.l
    c = pl.program_id(1)
    carried = ((xp_pool, OFF_POOL, POOL_BUF, npool_ref), (xp_sc, OFF_SC, SCONV_K - 1, nsc_ref),
               (xp_cf, OFF_CF, CCONV_K - 1, ncf_ref))

    @pl.when(c == 0)
    def _():
        for xp, off, _, _ in carried:
            xp[:, 0:off, :] = jnp.zeros((N_SLAB, off, LANES), f32)

    def write_o(val):
        o_ref[...] = val

    _mixer_body(lay, lambda: h_ref[...], write_o, c * l, *params, *scr)

    for xp, off, nb, _ in carried:
        xp[:, off - nb:off, :] = xp[:, off + l - nb:off + l, :]

    @pl.when(c == pl.num_programs(1) - 1)
    def _():
        for xp, off, nb, out_ref in carried:
            for j in range(N_SLAB):
                out_ref[0, :, j * LANES:(j + 1) * LANES] = xp[j, off - nb:off, :]

    _run_casts(srcs, dsts, scaled)


def _mixer_sample_kernel(h_ref, spool_ref, ssc_ref, scf_ref, *refs, lay):
    params = refs[:N_MIX_PARAMS]
    o_ref, npool_ref, nsc_ref, ncf_ref = refs[N_MIX_PARAMS:N_MIX_PARAMS + 4]
    scr = refs[N_MIX_PARAMS + 4:]
    xp_pool, xp_sc, xp_cf = scr[-3:]
    l = lay.l
    carried = ((xp_pool, OFF_POOL, POOL_BUF, spool_ref, npool_ref), (xp_sc, OFF_SC, SCONV_K - 1, ssc_ref, nsc_ref),
               (xp_cf, OFF_CF, CCONV_K - 1, scf_ref, ncf_ref))
    for xp, off, nb, in_ref, _ in carried:
        for j in range(N_SLAB):
            xp[j, off - nb:off, :, :] = in_ref[:, :, j * LANES:(j + 1) * LANES]
    assert PAST_LEN + 1 >= max(POOL_WINDOWS)

    def write_o(val):
        o_ref[...] = val.reshape(l, lay.ns, D_MODEL)

    _mixer_body(lay, lambda: h_ref[...].reshape(lay.tm, D_MODEL), write_o, None, *params, *scr)
    for xp, off, nb, _, out_ref in carried:
        for j in range(N_SLAB):
            out_ref[:, :, j * LANES:(j + 1) * LANES] = xp[j, off + l - nb:off + l, :, :]


def _mixer_prompt_call(h, params, casts, *, batch, seq, l):
    ch = seq // l
    lay = _OneSequence(l)
    row = pl.BlockSpec((l, D_MODEL), lambda b, c: (b * ch + c, 0))
    c_in, c_out, c_shapes, c_arrays, scaled = _cast_plan(casts, batch * ch, lambda b, c: b * ch + c)

    def state_spec(nb):
        return pl.BlockSpec((1, nb, W_BR), lambda b, c: (b, 0, 0))

    return pl.pallas_call(
        functools.partial(_mixer_prompt_kernel, lay=lay, n_cast=len(casts), scaled=scaled),
        grid=(batch, ch),
        in_specs=[row] + [_spec(a, i) for a, i in params] + c_in,
        out_specs=[row, state_spec(POOL_BUF), state_spec(SCONV_K - 1), state_spec(CCONV_K - 1)] + c_out,
        out_shape=[jax.ShapeDtypeStruct(h.shape, f32),
                   jax.ShapeDtypeStruct((batch, POOL_BUF, W_BR), f32),
                   jax.ShapeDtypeStruct((batch, SCONV_K - 1, W_BR), f32),
                   jax.ShapeDtypeStruct((batch, CCONV_K - 1, W_BR), f32)] + c_shapes,
        scratch_shapes=_mixer_scratch(lay),
        compiler_params=_params(ndim=2),
        name="mixer_prompt",
    )(h, *[a for a, _ in params], *c_arrays)


def _mixer_sample_call(h, i, spool, ssc, scf, params, *, ns):
    l, nseq, _ = h.shape
    lay = _StepMajor(l, ns)
    row = pl.BlockSpec((l, ns, D_MODEL), lambda s: (0, s, 0))

    def state_in(nb):
        return pl.BlockSpec((None, nb, ns, W_BR), lambda s: (i, 0, s, 0))

    def state_out(nb):
        return pl.BlockSpec((nb, ns, W_BR), lambda s: (0, s, 0))

    nbs = (POOL_BUF, SCONV_K - 1, CCONV_K - 1)
    return pl.pallas_call(
        functools.partial(_mixer_sample_kernel, lay=lay),
        grid=(nseq // ns,),
        in_specs=[row] + [state_in(nb) for nb in nbs] + [_spec(a, j) for a, j in params],
        out_specs=[row] + [state_out(nb) for nb in nbs],
        out_shape=[jax.ShapeDtypeStruct(h.shape, f32)] + [jax.ShapeDtypeStruct((nb, nseq, W_BR), f32) for nb in nbs],
        scratch_shapes=_mixer_scratch(lay),
        compiler_params=_params(),
        name="mixer_sample",
    )(h, spool, ssc, scf, *[a for a, _ in params])


def kernel(x_prompt, x_sample, p_prompt, p_sample, state_pool, state_sconv, state_cconv, g_f1_pre, w_f1_gate, w_f1_up, w_f1_down, g_f1_post, g_mix_pre, w_in, pool_w, pool_scale, w_pool_out, sc_conv, w_sc_out, cf_conv, cf_conv_b, cf_ln_g, cf_ln_b, w_cf_out, w_o, g_mix_post, g_f2_pre, w_f2_gate, w_f2_up, w_f2_down, g_f2_post, g_ple_pre, w_ple_gate, w_ple_proj, g_ple_post, g_final):
    depth = w_in.shape[0]
    batch, seq, _ = x_prompt.shape
    nseq, dec_seq, _ = x_sample.shape
    tm = 512
    tm_tok = 1024
    ns = 32

    def vec(a, i):
        return (a.reshape(depth, 1, a.shape[-1]), i)

    def whole(a):
        return (a, None)

    gfin = whole(g_final.reshape(1, -1))
    pool_w2 = pool_w.reshape(depth, W_BR, LANES)

    hp = x_prompt.reshape(batch * seq, D_MODEL)
    hs = x_sample.transpose(1, 0, 2).reshape(dec_seq * nseq, D_MODEL)
    pp = p_prompt.reshape(depth, batch * seq, PLE_DIM)
    ps = p_sample.transpose(0, 2, 1, 3).reshape(depth, dec_seq * nseq, PLE_DIM)
    spool, ssc, scf = (s.transpose(0, 2, 1, 3) for s in (state_pool, state_sconv, state_cconv))

    f1_w = [(w_f1_gate[0] * g_f1_pre[0][:, None]).astype(bf16), (w_f1_up[0] * g_f1_pre[0][:, None]).astype(bf16),
            w_f1_down[0].astype(bf16)]
    new_p, new_s = [], []
    for i in range(depth):
        f1 = [whole(w) for w in f1_w] + [vec(g_f1_post, i)]
        mix_src = (w_in, pool_w2, w_pool_out, w_sc_out, w_cf_out, w_o)
        hp, *mix_w = _ffn_call(hp, f1, [(w, i, None) for w in mix_src], tm=tm_tok)
        hs, = _ffn_call(hs, f1, tm=tm)
        wi, pw, wpo, wso, wco, wo = mix_w
        mix = [vec(g_mix_pre, i), whole(wi), whole(pw.reshape(len(POOL_WINDOWS), LANES, LANES)), vec(pool_scale, i),
               whole(wpo), (sc_conv, i), whole(wso), (cf_conv, i), vec(cf_conv_b, i), vec(cf_ln_g, i),
               vec(cf_ln_b, i), whole(wco), whole(wo), vec(g_mix_post, i)]
        f2_src = ((w_f2_gate, i, g_f2_pre), (w_f2_up, i, g_f2_pre), (w_f2_down, i, None),
                  (w_ple_gate, i, g_ple_pre), (w_ple_proj, i, None))
        hp, *rest = _mixer_prompt_call(hp, mix, f2_src, batch=batch, seq=seq, l=tm)
        st_p, f2_w = rest[:3], rest[3:]
        hs, *st_s = _mixer_sample_call(hs.reshape(dec_seq, nseq, D_MODEL), i, spool, ssc, scf, mix, ns=ns)
        hs = hs.reshape(dec_seq * nseq, D_MODEL)
        new_p.append(st_p)
        new_s.append(st_s)
        wg, wu, wd, wpg, wpp = f2_w
        f2 = [whole(wg), whole(wu), whole(wd), vec(g_f2_post, i), whole(wpg), whole(wpp), vec(g_ple_post, i), gfin]
        final = i == depth - 1
        nxt = [] if final else [(w_f1_gate, i + 1, g_f1_pre), (w_f1_up, i + 1, g_f1_pre), (w_f1_down, i + 1, None)]
        hp, *f1_w = _ffn_ple_call(hp, pp, i, f2, nxt, tm=tm_tok, final=final)
        hs, = _ffn_ple_call(hs, ps, i, f2, tm=tm, final=final)
    y_sample = hs.reshape(dec_seq, nseq, D_MODEL).transpose(1, 0, 2)
    stack_p = lambda k: jnp.stack([s[k] for s in new_p])
    stack_s = lambda k: jnp.stack([s[k] for s in new_s]).transpose(0, 2, 1, 3)
    return (hp.reshape(batch, seq, D_MODEL), y_sample,
            stack_p(0), stack_p(1), stack_p(2), stack_s(0), stack_s(1), stack_s(2))
```

```python
import functools

import jax
import jax.numpy as jnp
from jax import lax
from jax.experimental import pallas as pl
from jax.experimental.pallas import tpu as pltpu

D_MODEL = 1024
D_FF = 2816
PLE_DIM = 256
W_BR = 512
POOL_WINDOWS = (2, 4, 8, 16)
POOL_BUF = 15
SCONV_K = 3
CCONV_K = 31
PAST_LEN = 16384
EPS = 1e-6

LANES = 128
SUBLANES = 8
BF16_ROWS = 16
N_SLAB = W_BR // LANES
CONV_ROWS = 64

C_A, C_V, C_BG, C_CG, C_GA, C_GG, C_G0 = 0, 512, 1024, 1536, 2048, 2560, 3072
IN_COLS = C_G0 + 3 * D_MODEL
G_PART = 3 * D_MODEL // N_SLAB
C_F = C_GA
F_WIDTH = 2 * LANES + G_PART
IN_PERM = ((0, 0, C_GA),) + tuple(
    x for j in range(N_SLAB) for x in (
        (C_F + j * F_WIDTH, C_GA + j * LANES, LANES),
        (C_F + j * F_WIDTH + LANES, C_GG + j * LANES, LANES),
        (C_F + j * F_WIDTH + 2 * LANES, C_G0 + j * G_PART, G_PART)))

FF_CHUNKS = ((0, 1024), (1024, 1024), (2048, 768))
MIX_CHUNK = 512

VMEM_LIMIT = 56 * 1024 * 1024

f32 = jnp.float32
bf16 = jnp.bfloat16


def _round_up(x, m):
    return (x + m - 1) // m * m


def _rms(x, g):
    return x * lax.rsqrt(jnp.mean(x * x, axis=-1, keepdims=True) + EPS) * g


def _dot(a, b):
    return jnp.dot(a, b, preferred_element_type=f32)


def _resident(shape):
    nd = len(shape)
    return pl.BlockSpec(shape, lambda *_: (0,) * nd, pipeline_mode=pl.Buffered(1))


def _layer(shape, i):
    nd = len(shape)
    return pl.BlockSpec((None,) + tuple(shape[1:]), lambda *_: (i,) + (0,) * (nd - 1), pipeline_mode=pl.Buffered(1))


def _spec(a, i):
    return _resident(a.shape) if i is None else _layer(a.shape, i)


def _params(ndim=1):
    return pltpu.CompilerParams(dimension_semantics=("arbitrary",) * ndim, vmem_limit_bytes=VMEM_LIMIT)


def _cast_plan(casts, nsteps, lin):
    ins, gains, outs, shapes, arrays, gain_arrays, scaled, perms = [], [], [], [], [], [], [], {}
    for k, (src, i, gain, *perm) in enumerate(casts):
        if perm:
            perms[k] = perm[0]
        _, r, c = src.shape
        div = 1
        while (r * div) % nsteps or (r * div // nsteps) % BF16_ROWS:
            div *= 2
        br = r * div // nsteps
        ins.append(pl.BlockSpec((None, br, c), lambda *g, i=i, div=div: (i, lin(*g) // div, 0)))
        outs.append(pl.BlockSpec((br, c), lambda *g, div=div: (lin(*g) // div, 0)))
        shapes.append(jax.ShapeDtypeStruct((r, c), bf16))
        arrays.append(src)
        if gain is not None:
            gains.append(pl.BlockSpec((None, br, 1), lambda *g, i=i, div=div: (i, lin(*g) // div, 0)))
            gain_arrays.append(gain.reshape(gain.shape[0], r, 1))
            scaled.append(k)
    return ins + gains, outs, shapes, arrays + gain_arrays, (tuple(scaled), tuple(perms.items()))


def _run_casts(srcs, dsts, meta):
    scaled, perms = meta
    gains = dict(zip(scaled, srcs[len(dsts):]))
    perms = dict(perms)
    for k, d in enumerate(dsts):
        for dst, src, width in perms.get(k, ((0, 0, d.shape[-1]),)):
            w = srcs[k][:, src:src + width]
            if k in gains:
                w = w * gains[k][...]
            d[:, dst:dst + width] = w.astype(bf16)


def _row_scale(x):
    return lax.rsqrt(jnp.mean(x * x, axis=-1, keepdims=True) + EPS)


def _swiglu_into(acc_scr, u_scr, rs, wg_ref, wu_ref, wd_ref):
    for n, (c0, cw) in enumerate(FF_CHUNKS):
        u = u_scr[...]
        g = _dot(u, wg_ref[:, c0:c0 + cw]) * rs
        up = _dot(u, wu_ref[:, c0:c0 + cw]) * rs
        a = (jax.nn.silu(g) * up).astype(bf16)
        d = _dot(a, wd_ref[c0:c0 + cw, :])
        if n == 0:
            acc_scr[...] = d
        else:
            acc_scr[...] += d


def _split_refs(refs, n_in, n_out, meta):
    n_cast_in = n_out + len(meta[0])
    return (refs[:n_in], refs[n_in:n_in + n_cast_in], refs[n_in + n_cast_in],
            refs[n_in + n_cast_in + 1:n_in + n_cast_in + 1 + n_out], refs[n_in + n_cast_in + 1 + n_out:])


def _ffn_kernel(*refs, n_cast, meta):
    (h_ref, wg_ref, wu_ref, wd_ref, gpost_ref), srcs, o_ref, dsts, (u_scr, acc_scr) = _split_refs(refs, 5, n_cast, meta)
    h = h_ref[...]
    u_scr[...] = h.astype(bf16)
    _swiglu_into(acc_scr, u_scr, _row_scale(h), wg_ref, wu_ref, wd_ref)
    o_ref[...] = h_ref[...] + _rms(acc_scr[...], 0.5 * gpost_ref[...])
    _run_casts(srcs, dsts, meta)


def _ffn_ple_kernel(*refs, n_cast, meta, final):
    ((h_ref, p_ref, wg_ref, wu_ref, wd_ref, gpost_ref, wpg_ref, wpp_ref, gppost_ref, gfin_ref),
     srcs, o_ref, dsts, (u_scr, acc_scr)) = _split_refs(refs, 10, n_cast, meta)
    h = h_ref[...]
    u_scr[...] = h.astype(bf16)
    _swiglu_into(acc_scr, u_scr, _row_scale(h), wg_ref, wu_ref, wd_ref)
    h = h_ref[...] + _rms(acc_scr[...], 0.5 * gpost_ref[...])
    gate = jax.nn.sigmoid(_dot(h.astype(bf16), wpg_ref[...]) * _row_scale(h))
    proj = _dot(p_ref[...].astype(bf16), wpp_ref[...])
    h = h + _rms(proj * gate, gppost_ref[...])
    if final:
        h = _rms(h, gfin_ref[...])
    o_ref[...] = h
    _run_casts(srcs, dsts, meta)


def _token_call(kern, name, h, extra, extra_specs, params, casts, *, tm):
    steps = h.shape[0] // tm
    row = pl.BlockSpec((tm, D_MODEL), lambda r: (r, 0))
    c_in, c_out, c_shapes, c_arrays, meta = _cast_plan(casts, steps, lambda r: r)
    return pl.pallas_call(
        functools.partial(kern, n_cast=len(casts), meta=meta),
        grid=(steps,),
        in_specs=[row] + extra_specs + [_spec(a, i) for a, i in params] + c_in,
        out_specs=[row] + c_out,
        out_shape=[jax.ShapeDtypeStruct(h.shape, f32)] + c_shapes,
        scratch_shapes=[pltpu.VMEM((tm, D_MODEL), bf16), pltpu.VMEM((tm, D_MODEL), f32)],
        compiler_params=_params(),
        name=name,
    )(h, *extra, *[a for a, _ in params], *c_arrays)


def _ffn_call(h, params, casts=(), *, tm):
    return _token_call(_ffn_kernel, "ffn", h, [], [], params, casts, tm=tm)


def _ffn_ple_call(h, p, i, params, casts=(), *, tm, final):
    prow = pl.BlockSpec((None, tm, PLE_DIM), lambda r: (i, r, 0))
    return _token_call(functools.partial(_ffn_ple_kernel, final=final), "ffn_ple", h, [p], [prow], params, casts, tm=tm)


OFF_POOL = _round_up(POOL_BUF, SUBLANES)
OFF_SC = _round_up(SCONV_K - 1, SUBLANES)
OFF_CF = _round_up(CCONV_K - 1, SUBLANES)


class _OneSequence:
    def __init__(self, l):
        self.l, self.tm = l, l
        self.blocks = tuple(range(0, l, CONV_ROWS))
        self.row_parts = tuple((r0, l // 2) for r0 in (0, l // 2))

    def blocks_in(self, r0, nr):
        return tuple(range(r0, r0 + nr, CONV_ROWS))

    def part_to_slab(self, xp, off, j, r0, nr, val):
        xp[j, off + r0:off + r0 + nr, :] = val

    def scratch(self, cols, dtype):
        return pltpu.VMEM((self.l, cols), dtype)

    def slab(self, off):
        return pltpu.VMEM((N_SLAB, off + self.l, LANES), f32)

    def load(self, ref, cs=slice(None)):
        return ref[:, cs]

    def store(self, ref, cs, val):
        ref[:, cs] = val

    def blk_load(self, ref, b, cs):
        return ref[b:b + CONV_ROWS, cs]

    def blk_store(self, ref, b, cs, val):
        ref[b:b + CONV_ROWS, cs] = val

    def win(self, xp, j, start, b):
        return xp[j, start + b:start + b + CONV_ROWS, :]

    def conv(self, xp, j, start, b, taps):
        acc = None
        for k, w in enumerate(taps):
            term = self.win(xp, j, start + k, b) * w
            acc = term if acc is None else acc + term
        return acc

    def tap(self, w_ref, k, cs):
        return w_ref[k:k + 1, cs]

    def to_slab(self, xp, off, j, val):
        xp[j, off:off + self.l, :] = val

    def blk2d(self, val):
        return val

    def blk_like(self, val2d):
        return val2d

    def row_index(self, b):
        return b + lax.broadcasted_iota(jnp.int32, (CONV_ROWS, LANES), 0)


class _StepMajor:
    def __init__(self, l, ns):
        self.l, self.ns, self.tm = l, ns, l * ns
        self.blocks = tuple(range(0, ns, BF16_ROWS))
        self.row_parts = ((0, self.tm),)

    def blocks_in(self, r0, nr):
        return self.blocks

    def part_to_slab(self, xp, off, j, r0, nr, val):
        self.to_slab(xp, off, j, val)

    def scratch(self, cols, dtype):
        return pltpu.VMEM((self.l, self.ns, cols), dtype)

    def slab(self, off):
        return pltpu.VMEM((N_SLAB, off + self.l, self.ns, LANES), f32)

    def load(self, ref, cs=slice(None)):
        v = ref[:, :, cs]
        return v.reshape(self.tm, v.shape[-1])

    def store(self, ref, cs, val):
        ref[:, :, cs] = val.reshape(self.l, self.ns, val.shape[-1])

    def blk_load(self, ref, b, cs):
        return ref[:, b:b + BF16_ROWS, cs]

    def blk_store(self, ref, b, cs, val):
        ref[:, b:b + BF16_ROWS, cs] = val

    def win(self, xp, j, start, b):
        return xp[j, start:start + self.l, b:b + BF16_ROWS, :]

    def conv(self, xp, j, start, b, taps):
        acc = None
        for k, w in enumerate(taps):
            term = self.win(xp, j, start + k, b) * w
            acc = term if acc is None else acc + term
        return acc

    def tap(self, w_ref, k, cs):
        return w_ref[k:k + 1, cs].reshape(1, 1, LANES)

    def to_slab(self, xp, off, j, val):
        xp[j, off:off + self.l, :, :] = val.reshape(self.l, self.ns, LANES)

    def blk2d(self, val):
        return val.reshape(self.l * BF16_ROWS, LANES)

    def blk_like(self, val2d):
        return val2d.reshape(self.l, BF16_ROWS, LANES)


def _mixer_body(lay, read_h, write_o, pos0,
                gpre_ref, win_ref, poolw_ref, pscale_ref, wpo_ref, scw_ref, wso_ref, cfw_ref, cfb_ref,
                lng_ref, lnb_ref, wco_ref, wo_ref, gpost_ref,
                u_scr, zb_scr, g_scr, y_scr, cc_scr, xp_pool, xp_sc, xp_cf):
    def _project_pool_shortconv():
        u_scr[...] = _rms(read_h(), gpre_ref[...]).astype(bf16)
        lay.store(zb_scr, slice(None), _dot(u_scr[...], win_ref[:, C_V:C_V + 3 * W_BR]))
        for j in range(N_SLAB):
            v = lay.load(zb_scr, slice(j * LANES, (j + 1) * LANES))
            cg = lay.load(zb_scr, slice(2 * W_BR + j * LANES, 2 * W_BR + (j + 1) * LANES))
            lay.to_slab(xp_sc, OFF_SC, j, cg * v)
        za = _dot(u_scr[...], win_ref[:, C_A:C_A + W_BR])
        for j in range(N_SLAB):
            lay.to_slab(xp_pool, OFF_POOL, j, za[:, j * LANES:(j + 1) * LANES])

        for j in range(N_SLAB):
            cs = slice(j * LANES, (j + 1) * LANES)
            taps = [lay.tap(scw_ref, k, cs) for k in range(SCONV_K)]
            for b in lay.blocks:
                cv = lay.conv(xp_sc, j, OFF_SC - (SCONV_K - 1), b, taps)
                bg = lay.blk_load(zb_scr, b, slice(W_BR + j * LANES, W_BR + (j + 1) * LANES))
                lay.blk_store(y_scr, b, slice(W_BR + j * LANES, W_BR + (j + 1) * LANES), (bg * cv).astype(bf16))

        for j, w in enumerate(POOL_WINDOWS):
            cs = slice(j * LANES, (j + 1) * LANES)
            for b in lay.blocks:
                cur = lay.win(xp_pool, j, OFF_POOL, b)
                s = cur
                for i in range(1, w):
                    s = s + lay.win(xp_pool, j, OFF_POOL - i, b)
                if pos0 is not None and b < w - 1:
                    mean = s / jnp.minimum(w, pos0 + lay.row_index(b) + 1).astype(f32)
                else:
                    mean = s * (1.0 / w)
                mm = lay.blk2d(mean - cur).astype(bf16)
                ya = _dot(mm, poolw_ref[j]) * pscale_ref[:, cs]
                lay.blk_store(y_scr, b, cs, lay.blk_like(ya.astype(bf16)))

    def _gates_and_conformer_conv():
        for j in range(N_SLAB):
            cs = slice(j * LANES, (j + 1) * LANES)
            taps = [lay.tap(cfw_ref, k, cs) for k in range(CCONV_K)]
            bias = cfb_ref[:, cs]
            for r0, nr in lay.row_parts:
                z = _dot(u_scr[r0:r0 + nr, :], win_ref[:, C_F + j * F_WIDTH:C_F + (j + 1) * F_WIDTH])
                glu = z[:, 0:LANES] * jax.nn.sigmoid(z[:, LANES:2 * LANES])
                lay.part_to_slab(xp_cf, OFF_CF, j, r0, nr, glu)
                g_scr[r0:r0 + nr, j * G_PART:(j + 1) * G_PART] = jax.nn.sigmoid(z[:, 2 * LANES:])
                for b in lay.blocks_in(r0, nr):
                    cc = lay.conv(xp_cf, j, OFF_CF - (CCONV_K - 1), b, taps)
                    lay.blk_store(cc_scr, b, cs, cc + bias)
        x = lay.load(cc_scr)
        xc = x - jnp.mean(x, axis=-1, keepdims=True)
        ln = xc * lax.rsqrt(jnp.mean(xc * xc, axis=-1, keepdims=True) + EPS) * lng_ref[...] + lnb_ref[...]
        lay.store(y_scr, slice(2 * W_BR, 3 * W_BR), jax.nn.silu(ln).astype(bf16))

    def _combine_and_project():
        m = []
        for c0 in range(0, D_MODEL, MIX_CHUNK):
            cs = slice(c0, c0 + MIX_CHUNK)
            acc = None
            for b, w_ref in enumerate((wpo_ref, wso_ref, wco_ref)):
                yb = lay.load(y_scr, slice(b * W_BR, (b + 1) * W_BR))
                term = g_scr[:, b * D_MODEL + c0:b * D_MODEL + c0 + MIX_CHUNK] * _dot(yb, w_ref[:, cs])
                acc = term if acc is None else acc + term
            m.append(acc.astype(bf16))
        mo = _dot(jnp.concatenate(m, axis=-1), wo_ref[...])
        write_o(read_h() + _rms(mo, gpost_ref[...]))

    _project_pool_shortconv()
    _gates_and_conformer_conv()
    _combine_and_project()


N_MIX_PARAMS = 14


def _mixer_scratch(lay):
    tm = lay.tm
    return [
        pltpu.VMEM((tm, D_MODEL), bf16),
        lay.scratch(3 * W_BR, f32),
        pltpu.VMEM((tm, 3 * D_MODEL), f32),
        lay.scratch(3 * W_BR, bf16),
        lay.scratch(W_BR, f32),
        lay.slab(OFF_POOL), lay.slab(OFF_SC), lay.slab(OFF_CF),
    ]


def _mixer_prompt_kernel(h_ref, *refs, lay, n_cast, meta):
    params = refs[:N_MIX_PARAMS]
    n_src = n_cast + len(meta[0])
    srcs = refs[N_MIX_PARAMS:N_MIX_PARAMS + n_src]
    o_ref, npool_ref, nsc_ref, ncf_ref = refs[N_MIX_PARAMS + n_src:N_MIX_PARAMS + n_src + 4]
    dsts = refs[N_MIX_PARAMS + n_src + 4:N_MIX_PARAMS + n_src + n_cast + 4]
    scr = refs[N_MIX_PARAMS + n_src + n_cast + 4:]
    xp_pool, xp_sc, xp_cf = scr[-3:]
    l = lay.l
    c = pl.program_id(1)
    carried = ((xp_pool, OFF_POOL, POOL_BUF, npool_ref), (xp_sc, OFF_SC, SCONV_K - 1, nsc_ref),
               (xp_cf, OFF_CF, CCONV_K - 1, ncf_ref))

    @pl.when(c == 0)
    def _():
        for xp, off, _, _ in carried:
            xp[:, 0:off, :] = jnp.zeros((N_SLAB, off, LANES), f32)

    def write_o(val):
        o_ref[...] = val

    _mixer_body(lay, lambda: h_ref[...], write_o, c * l, *params, *scr)

    for xp, off, nb, _ in carried:
        xp[:, off - nb:off, :] = xp[:, off + l - nb:off + l, :]

    @pl.when(c == pl.num_programs(1) - 1)
    def _():
        for xp, off, nb, out_ref in carried:
            for j in range(N_SLAB):
                out_ref[0, :, j * LANES:(j + 1) * LANES] = xp[j, off - nb:off, :]

    _run_casts(srcs, dsts, meta)


def _mixer_sample_kernel(h_ref, spool_ref, ssc_ref, scf_ref, *refs, lay):
    params = refs[:N_MIX_PARAMS]
    o_ref, npool_ref, nsc_ref, ncf_ref = refs[N_MIX_PARAMS:N_MIX_PARAMS + 4]
    scr = refs[N_MIX_PARAMS + 4:]
    xp_pool, xp_sc, xp_cf = scr[-3:]
    l = lay.l
    carried = ((xp_pool, OFF_POOL, POOL_BUF, spool_ref, npool_ref), (xp_sc, OFF_SC, SCONV_K - 1, ssc_ref, nsc_ref),
               (xp_cf, OFF_CF, CCONV_K - 1, scf_ref, ncf_ref))
    for xp, off, nb, in_ref, _ in carried:
        for j in range(N_SLAB):
            xp[j, off - nb:off, :, :] = in_ref[:, :, j * LANES:(j + 1) * LANES]
    assert PAST_LEN + 1 >= max(POOL_WINDOWS)

    def write_o(val):
        o_ref[...] = val.reshape(l, lay.ns, D_MODEL)

    _mixer_body(lay, lambda: h_ref[...].reshape(lay.tm, D_MODEL), write_o, None, *params, *scr)
    for xp, off, nb, _, out_ref in carried:
        for j in range(N_SLAB):
            out_ref[:, :, j * LANES:(j + 1) * LANES] = xp[j, off + l - nb:off + l, :, :]


def _mixer_prompt_call(h, params, casts, *, batch, seq, l):
    ch = seq // l
    lay = _OneSequence(l)
    row = pl.BlockSpec((l, D_MODEL), lambda b, c: (b * ch + c, 0))
    c_in, c_out, c_shapes, c_arrays, meta = _cast_plan(casts, batch * ch, lambda b, c: b * ch + c)

    def state_spec(nb):
        return pl.BlockSpec((1, nb, W_BR), lambda b, c: (b, 0, 0))

    return pl.pallas_call(
        functools.partial(_mixer_prompt_kernel, lay=lay, n_cast=len(casts), meta=meta),
        grid=(batch, ch),
        in_specs=[row] + [_spec(a, i) for a, i in params] + c_in,
        out_specs=[row, state_spec(POOL_BUF), state_spec(SCONV_K - 1), state_spec(CCONV_K - 1)] + c_out,
        out_shape=[jax.ShapeDtypeStruct(h.shape, f32),
                   jax.ShapeDtypeStruct((batch, POOL_BUF, W_BR), f32),
                   jax.ShapeDtypeStruct((batch, SCONV_K - 1, W_BR), f32),
                   jax.ShapeDtypeStruct((batch, CCONV_K - 1, W_BR), f32)] + c_shapes,
        scratch_shapes=_mixer_scratch(lay),
        compiler_params=_params(ndim=2),
        name="mixer_prompt",
    )(h, *[a for a, _ in params], *c_arrays)


def _mixer_sample_call(h, i, spool, ssc, scf, params, *, ns):
    l, nseq, _ = h.shape
    lay = _StepMajor(l, ns)
    row = pl.BlockSpec((l, ns, D_MODEL), lambda s: (0, s, 0))

    def state_in(nb):
        return pl.BlockSpec((None, nb, ns, W_BR), lambda s: (i, 0, s, 0))

    def state_out(nb):
        return pl.BlockSpec((nb, ns, W_BR), lambda s: (0, s, 0))

    nbs = (POOL_BUF, SCONV_K - 1, CCONV_K - 1)
    return pl.pallas_call(
        functools.partial(_mixer_sample_kernel, lay=lay),
        grid=(nseq // ns,),
        in_specs=[row] + [state_in(nb) for nb in nbs] + [_spec(a, j) for a, j in params],
        out_specs=[row] + [state_out(nb) for nb in nbs],
        out_shape=[jax.ShapeDtypeStruct(h.shape, f32)] + [jax.ShapeDtypeStruct((nb, nseq, W_BR), f32) for nb in nbs],
        scratch_shapes=_mixer_scratch(lay),
        compiler_params=_params(),
        name="mixer_sample",
    )(h, spool, ssc, scf, *[a for a, _ in params])


def kernel(x_prompt, x_sample, p_prompt, p_sample, state_pool, state_sconv, state_cconv, g_f1_pre, w_f1_gate, w_f1_up, w_f1_down, g_f1_post, g_mix_pre, w_in, pool_w, pool_scale, w_pool_out, sc_conv, w_sc_out, cf_conv, cf_conv_b, cf_ln_g, cf_ln_b, w_cf_out, w_o, g_mix_post, g_f2_pre, w_f2_gate, w_f2_up, w_f2_down, g_f2_post, g_ple_pre, w_ple_gate, w_ple_proj, g_ple_post, g_final):
    depth = w_in.shape[0]
    batch, seq, _ = x_prompt.shape
    nseq, dec_seq, _ = x_sample.shape
    tm = 512
    tm_tok = 1024
    ns = 32

    def vec(a, i):
        return (a.reshape(depth, 1, a.shape[-1]), i)

    def whole(a):
        return (a, None)

    gfin = whole(g_final.reshape(1, -1))
    pool_w2 = pool_w.reshape(depth, W_BR, LANES)

    hp = x_prompt.reshape(batch * seq, D_MODEL)
    hs = x_sample.transpose(1, 0, 2).reshape(dec_seq * nseq, D_MODEL)
    pp = p_prompt.reshape(depth, batch * seq, PLE_DIM)
    ps = p_sample.transpose(0, 2, 1, 3).reshape(depth, dec_seq * nseq, PLE_DIM)
    spool, ssc, scf = (s.transpose(0, 2, 1, 3) for s in (state_pool, state_sconv, state_cconv))

    f1_w = [(w_f1_gate[0] * g_f1_pre[0][:, None]).astype(bf16), (w_f1_up[0] * g_f1_pre[0][:, None]).astype(bf16),
            w_f1_down[0].astype(bf16)]
    new_p, new_s = [], []
    for i in range(depth):
        f1 = [whole(w) for w in f1_w] + [vec(g_f1_post, i)]
        mix_src = (w_in, pool_w2, w_pool_out, w_sc_out, w_cf_out, w_o)
        hp, *mix_w = _ffn_call(hp, f1, [(w_in, i, None, IN_PERM)] + [(w, i, None) for w in mix_src[1:]], tm=tm_tok)
        hs, = _ffn_call(hs, f1, tm=tm)
        wi, pw, wpo, wso, wco, wo = mix_w
        mix = [vec(g_mix_pre, i), whole(wi), whole(pw.reshape(len(POOL_WINDOWS), LANES, LANES)), vec(pool_scale, i),
               whole(wpo), (sc_conv, i), whole(wso), (cf_conv, i), vec(cf_conv_b, i), vec(cf_ln_g, i),
               vec(cf_ln_b, i), whole(wco), whole(wo), vec(g_mix_post, i)]
        f2_src = ((w_f2_gate, i, g_f2_pre), (w_f2_up, i, g_f2_pre), (w_f2_down, i, None),
                  (w_ple_gate, i, g_ple_pre), (w_ple_proj, i, None))
        hp, *rest = _mixer_prompt_call(hp, mix, f2_src, batch=batch, seq=seq, l=tm)
        st_p, f2_w = rest[:3], rest[3:]
        hs, *st_s = _mixer_sample_call(hs.reshape(dec_seq, nseq, D_MODEL), i, spool, ssc, scf, mix, ns=ns)
        hs = hs.reshape(dec_seq * nseq, D_MODEL)
        new_p.append(st_p)
        new_s.append(st_s)
        wg, wu, wd, wpg, wpp = f2_w
        f2 = [whole(wg), whole(wu), whole(wd), vec(g_f2_post, i), whole(wpg), whole(wpp), vec(g_ple_post, i), gfin]
        final = i == depth - 1
        nxt = [] if final else [(w_f1_gate, i + 1, g_f1_pre), (w_f1_up, i + 1, g_f1_pre), (w_f1_down, i + 1, None)]
        hp, *f1_w = _ffn_ple_call(hp, pp, i, f2, nxt, tm=tm_tok, final=final)
        hs, = _ffn_ple_call(hs, ps, i, f2, tm=tm, final=final)
    y_sample = hs.reshape(dec_seq, nseq, D_MODEL).transpose(1, 0, 2)
    stack_p = lambda k: jnp.stack([s[k] for s in new_p])
    stack_s = lambda k: jnp.stack([s[k] for s in new_s]).transpose(0, 2, 1, 3)
    return (hp.reshape(batch, seq, D_MODEL), y_sample,
            stack_p(0), stack_p(1), stack_p(2), stack_s(0), stack_s(1), stack_s(2))
```

```python
import functools

import jax
import jax.numpy as jnp
from jax import lax
from jax.experimental import pallas as pl
from jax.experimental.pallas import tpu as pltpu

D_MODEL = 1024
D_FF = 2816
PLE_DIM = 256
W_BR = 512
POOL_WINDOWS = (2, 4, 8, 16)
POOL_BUF = 15
SCONV_K = 3
CCONV_K = 31
PAST_LEN = 16384
EPS = 1e-6

LANES = 128
SUBLANES = 8
BF16_ROWS = 16
N_SLAB = W_BR // LANES
CONV_ROWS = 64

C_A, C_V, C_BG, C_CG, C_GA, C_GG, C_G0 = 0, 512, 1024, 1536, 2048, 2560, 3072
IN_COLS = C_G0 + 3 * D_MODEL
G_PART = 3 * D_MODEL // N_SLAB
C_F = C_GA
F_WIDTH = 2 * LANES + G_PART
IN_PERM = ((0, 0, C_GA),) + tuple(
    x for j in range(N_SLAB) for x in (
        (C_F + j * F_WIDTH, C_GA + j * LANES, LANES),
        (C_F + j * F_WIDTH + LANES, C_GG + j * LANES, LANES),
        (C_F + j * F_WIDTH + 2 * LANES, C_G0 + j * G_PART, G_PART)))

FF_CHUNKS = ((0, 1024), (1024, 1024), (2048, 768))
MIX_CHUNK = 512

VMEM_LIMIT = 56 * 1024 * 1024

f32 = jnp.float32
bf16 = jnp.bfloat16


def _round_up(x, m):
    return (x + m - 1) // m * m


def _rms(x, g):
    return x * lax.rsqrt(jnp.mean(x * x, axis=-1, keepdims=True) + EPS) * g


def _dot(a, b):
    return jnp.dot(a, b, preferred_element_type=f32)


def _resident(shape):
    nd = len(shape)
    return pl.BlockSpec(shape, lambda *_: (0,) * nd, pipeline_mode=pl.Buffered(1))


def _layer(shape, i):
    nd = len(shape)
    return pl.BlockSpec((None,) + tuple(shape[1:]), lambda *_: (i,) + (0,) * (nd - 1), pipeline_mode=pl.Buffered(1))


def _spec(a, i):
    return _resident(a.shape) if i is None else _layer(a.shape, i)


def _params(ndim=1):
    return pltpu.CompilerParams(dimension_semantics=("arbitrary",) * ndim, vmem_limit_bytes=VMEM_LIMIT)


def _cast_plan(casts, nsteps, lin):
    ins, gains, outs, shapes, arrays, gain_arrays, scaled, perms = [], [], [], [], [], [], [], {}
    for k, (src, i, gain, *perm) in enumerate(casts):
        if perm:
            perms[k] = perm[0]
        _, r, c = src.shape
        div = 1
        while (r * div) % nsteps or (r * div // nsteps) % BF16_ROWS:
            div *= 2
        br = r * div // nsteps
        ins.append(pl.BlockSpec((None, br, c), lambda *g, i=i, div=div: (i, lin(*g) // div, 0)))
        outs.append(pl.BlockSpec((br, c), lambda *g, div=div: (lin(*g) // div, 0)))
        shapes.append(jax.ShapeDtypeStruct((r, c), bf16))
        arrays.append(src)
        if gain is not None:
            gains.append(pl.BlockSpec((None, br, 1), lambda *g, i=i, div=div: (i, lin(*g) // div, 0)))
            gain_arrays.append(gain.reshape(gain.shape[0], r, 1))
            scaled.append(k)
    return ins + gains, outs, shapes, arrays + gain_arrays, (tuple(scaled), tuple(perms.items()))


def _run_casts(srcs, dsts, meta):
    scaled, perms = meta
    gains = dict(zip(scaled, srcs[len(dsts):]))
    perms = dict(perms)
    for k, d in enumerate(dsts):
        for dst, src, width in perms.get(k, ((0, 0, d.shape[-1]),)):
            w = srcs[k][:, src:src + width]
            if k in gains:
                w = w * gains[k][...]
            d[:, dst:dst + width] = w.astype(bf16)


def _row_scale(x):
    return lax.rsqrt(jnp.mean(x * x, axis=-1, keepdims=True) + EPS)


def _swiglu_into(acc_scr, u_scr, rs, wg_ref, wu_ref, wd_ref):
    for n, (c0, cw) in enumerate(FF_CHUNKS):
        u = u_scr[...]
        g = _dot(u, wg_ref[:, c0:c0 + cw]) * rs
        up = _dot(u, wu_ref[:, c0:c0 + cw]) * rs
        a = (jax.nn.silu(g) * up).astype(bf16)
        d = _dot(a, wd_ref[c0:c0 + cw, :])
        if n == 0:
            acc_scr[...] = d
        else:
            acc_scr[...] += d


def _split_refs(refs, n_in, n_out, meta):
    n_cast_in = n_out + len(meta[0])
    return (refs[:n_in], refs[n_in:n_in + n_cast_in], refs[n_in + n_cast_in],
            refs[n_in + n_cast_in + 1:n_in + n_cast_in + 1 + n_out], refs[n_in + n_cast_in + 1 + n_out:])


def _ffn_kernel(*refs, n_cast, meta):
    (h_ref, wg_ref, wu_ref, wd_ref, gpost_ref), srcs, o_ref, dsts, (u_scr, acc_scr) = _split_refs(refs, 5, n_cast, meta)
    h = h_ref[...]
    u_scr[...] = h.astype(bf16)
    _swiglu_into(acc_scr, u_scr, _row_scale(h), wg_ref, wu_ref, wd_ref)
    o_ref[...] = h_ref[...] + _rms(acc_scr[...], 0.5 * gpost_ref[...])
    _run_casts(srcs, dsts, meta)


def _ffn_ple_kernel(*refs, n_cast, meta, final):
    ((h_ref, p_ref, wg_ref, wu_ref, wd_ref, gpost_ref, wpg_ref, wpp_ref, gppost_ref, gfin_ref),
     srcs, o_ref, dsts, (u_scr, acc_scr)) = _split_refs(refs, 10, n_cast, meta)
    h = h_ref[...]
    u_scr[...] = h.astype(bf16)
    _swiglu_into(acc_scr, u_scr, _row_scale(h), wg_ref, wu_ref, wd_ref)
    h = h_ref[...] + _rms(acc_scr[...], 0.5 * gpost_ref[...])
    gate = jax.nn.sigmoid(_dot(h.astype(bf16), wpg_ref[...]) * _row_scale(h))
    proj = _dot(p_ref[...].astype(bf16), wpp_ref[...])
    h = h + _rms(proj * gate, gppost_ref[...])
    if final:
        h = _rms(h, gfin_ref[...])
    o_ref[...] = h
    _run_casts(srcs, dsts, meta)


def _token_call(kern, name, h, extra, extra_specs, params, casts, *, tm):
    steps = h.shape[0] // tm
    row = pl.BlockSpec((tm, D_MODEL), lambda r: (r, 0))
    c_in, c_out, c_shapes, c_arrays, meta = _cast_plan(casts, steps, lambda r: r)
    return pl.pallas_call(
        functools.partial(kern, n_cast=len(casts), meta=meta),
        grid=(steps,),
        in_specs=[row] + extra_specs + [_spec(a, i) for a, i in params] + c_in,
        out_specs=[row] + c_out,
        out_shape=[jax.ShapeDtypeStruct(h.shape, f32)] + c_shapes,
        scratch_shapes=[pltpu.VMEM((tm, D_MODEL), bf16), pltpu.VMEM((tm, D_MODEL), f32)],
        compiler_params=_params(),
        name=name,
    )(h, *extra, *[a for a, _ in params], *c_arrays)


def _ffn_call(h, params, casts=(), *, tm):
    return _token_call(_ffn_kernel, "ffn", h, [], [], params, casts, tm=tm)


def _ffn_ple_call(h, p, i, params, casts=(), *, tm, final):
    prow = pl.BlockSpec((None, tm, PLE_DIM), lambda r: (i, r, 0))
    return _token_call(functools.partial(_ffn_ple_kernel, final=final), "ffn_ple", h, [p], [prow], params, casts, tm=tm)


OFF_POOL = _round_up(POOL_BUF, SUBLANES)
OFF_SC = _round_up(SCONV_K - 1, SUBLANES)
OFF_CF = _round_up(CCONV_K - 1, SUBLANES)


class _OneSequence:
    def __init__(self, l):
        self.l, self.tm = l, l
        self.blocks = tuple(range(0, l, CONV_ROWS))
        self.row_parts = tuple((r0, l // 2) for r0 in (0, l // 2))

    def blocks_in(self, r0, nr):
        return tuple(range(r0, r0 + nr, CONV_ROWS))

    def part_to_slab(self, xp, off, j, r0, nr, val):
        xp[j, off + r0:off + r0 + nr, :] = val

    def scratch(self, cols, dtype):
        return pltpu.VMEM((self.l, cols), dtype)

    def slab(self, off):
        return pltpu.VMEM((N_SLAB, off + self.l, LANES), f32)

    def load(self, ref, cs=slice(None)):
        return ref[:, cs]

    def store(self, ref, cs, val):
        ref[:, cs] = val

    def blk_load(self, ref, b, cs):
        return ref[b:b + CONV_ROWS, cs]

    def blk_store(self, ref, b, cs, val):
        ref[b:b + CONV_ROWS, cs] = val

    def win(self, xp, j, start, b):
        return xp[j, start + b:start + b + CONV_ROWS, :]

    def conv(self, xp, j, start, b, taps):
        acc = None
        for k, w in enumerate(taps):
            term = self.win(xp, j, start + k, b) * w
            acc = term if acc is None else acc + term
        return acc

    def tap(self, w_ref, k, cs):
        return w_ref[k:k + 1, cs]

    def to_slab(self, xp, off, j, val):
        xp[j, off:off + self.l, :] = val

    def row_index(self, b):
        return b + lax.broadcasted_iota(jnp.int32, (CONV_ROWS, LANES), 0)


class _StepMajor:
    def __init__(self, l, ns):
        self.l, self.ns, self.tm = l, ns, l * ns
        self.blocks = tuple(range(0, ns, BF16_ROWS))
        self.row_parts = ((0, self.tm),)

    def blocks_in(self, r0, nr):
        return self.blocks

    def part_to_slab(self, xp, off, j, r0, nr, val):
        self.to_slab(xp, off, j, val)

    def scratch(self, cols, dtype):
        return pltpu.VMEM((self.l, self.ns, cols), dtype)

    def slab(self, off):
        return pltpu.VMEM((N_SLAB, off + self.l, self.ns, LANES), f32)

    def load(self, ref, cs=slice(None)):
        v = ref[:, :, cs]
        return v.reshape(self.tm, v.shape[-1])

    def store(self, ref, cs, val):
        ref[:, :, cs] = val.reshape(self.l, self.ns, val.shape[-1])

    def blk_load(self, ref, b, cs):
        return ref[:, b:b + BF16_ROWS, cs]

    def blk_store(self, ref, b, cs, val):
        ref[:, b:b + BF16_ROWS, cs] = val

    def win(self, xp, j, start, b):
        return xp[j, start:start + self.l, b:b + BF16_ROWS, :]

    def conv(self, xp, j, start, b, taps):
        acc = None
        for k, w in enumerate(taps):
            term = self.win(xp, j, start + k, b) * w
            acc = term if acc is None else acc + term
        return acc

    def tap(self, w_ref, k, cs):
        return w_ref[k:k + 1, cs].reshape(1, 1, LANES)

    def to_slab(self, xp, off, j, val):
        xp[j, off:off + self.l, :, :] = val.reshape(self.l, self.ns, LANES)


def _mixer_body(lay, read_h, write_o, pos0,
                gpre_ref, win_ref, poolw_ref, pscale_ref, wpo_ref, scw_ref, wso_ref, cfw_ref, cfb_ref,
                lng_ref, lnb_ref, wco_ref, wo_ref, gpost_ref,
                u_scr, zb_scr, g_scr, y_scr, mm_scr, cc_scr, xp_pool, xp_sc, xp_cf):
    def _project_pool_shortconv():
        u_scr[...] = _rms(read_h(), gpre_ref[...]).astype(bf16)
        lay.store(zb_scr, slice(None), _dot(u_scr[...], win_ref[:, C_V:C_V + 3 * W_BR]))
        for j in range(N_SLAB):
            v = lay.load(zb_scr, slice(j * LANES, (j + 1) * LANES))
            cg = lay.load(zb_scr, slice(2 * W_BR + j * LANES, 2 * W_BR + (j + 1) * LANES))
            lay.to_slab(xp_sc, OFF_SC, j, cg * v)
        za = _dot(u_scr[...], win_ref[:, C_A:C_A + W_BR])
        for j in range(N_SLAB):
            lay.to_slab(xp_pool, OFF_POOL, j, za[:, j * LANES:(j + 1) * LANES])

        for j in range(N_SLAB):
            cs = slice(j * LANES, (j + 1) * LANES)
            taps = [lay.tap(scw_ref, k, cs) for k in range(SCONV_K)]
            for b in lay.blocks:
                cv = lay.conv(xp_sc, j, OFF_SC - (SCONV_K - 1), b, taps)
                bg = lay.blk_load(zb_scr, b, slice(W_BR + j * LANES, W_BR + (j + 1) * LANES))
                lay.blk_store(y_scr, b, slice(W_BR + j * LANES, W_BR + (j + 1) * LANES), (bg * cv).astype(bf16))

        for j, w in enumerate(POOL_WINDOWS):
            cs = slice(j * LANES, (j + 1) * LANES)
            for b in lay.blocks:
                cur = lay.win(xp_pool, j, OFF_POOL, b)
                s = cur
                for i in range(1, w):
                    s = s + lay.win(xp_pool, j, OFF_POOL - i, b)
                if pos0 is not None and b < w - 1:
                    mean = s / jnp.minimum(w, pos0 + lay.row_index(b) + 1).astype(f32)
                else:
                    mean = s * (1.0 / w)
                lay.blk_store(mm_scr, b, cs, (mean - cur).astype(bf16))
            ya = _dot(lay.load(mm_scr, cs), poolw_ref[j]) * pscale_ref[:, cs]
            lay.store(y_scr, cs, ya.astype(bf16))

    def _gates_and_conformer_conv():
        for j in range(N_SLAB):
            cs = slice(j * LANES, (j + 1) * LANES)
            taps = [lay.tap(cfw_ref, k, cs) for k in range(CCONV_K)]
            bias = cfb_ref[:, cs]
            for r0, nr in lay.row_parts:
                z = _dot(u_scr[r0:r0 + nr, :], win_ref[:, C_F + j * F_WIDTH:C_F + (j + 1) * F_WIDTH])
                glu = z[:, 0:LANES] * jax.nn.sigmoid(z[:, LANES:2 * LANES])
                lay.part_to_slab(xp_cf, OFF_CF, j, r0, nr, glu)
                g_scr[r0:r0 + nr, j * G_PART:(j + 1) * G_PART] = jax.nn.sigmoid(z[:, 2 * LANES:])
                for b in lay.blocks_in(r0, nr):
                    cc = lay.conv(xp_cf, j, OFF_CF - (CCONV_K - 1), b, taps)
                    lay.blk_store(cc_scr, b, cs, cc + bias)
        x = lay.load(cc_scr)
        xc = x - jnp.mean(x, axis=-1, keepdims=True)
        ln = xc * lax.rsqrt(jnp.mean(xc * xc, axis=-1, keepdims=True) + EPS) * lng_ref[...] + lnb_ref[...]
        lay.store(y_scr, slice(2 * W_BR, 3 * W_BR), jax.nn.silu(ln).astype(bf16))

    def _combine_and_project():
        m = []
        for c0 in range(0, D_MODEL, MIX_CHUNK):
            cs = slice(c0, c0 + MIX_CHUNK)
            acc = None
            for b, w_ref in enumerate((wpo_ref, wso_ref, wco_ref)):
                yb = lay.load(y_scr, slice(b * W_BR, (b + 1) * W_BR))
                term = g_scr[:, b * D_MODEL + c0:b * D_MODEL + c0 + MIX_CHUNK] * _dot(yb, w_ref[:, cs])
                acc = term if acc is None else acc + term
            m.append(acc.astype(bf16))
        mo = _dot(jnp.concatenate(m, axis=-1), wo_ref[...])
        write_o(read_h() + _rms(mo, gpost_ref[...]))

    _project_pool_shortconv()
    _gates_and_conformer_conv()
    _combine_and_project()


N_MIX_PARAMS = 14


def _mixer_scratch(lay):
    tm = lay.tm
    return [
        pltpu.VMEM((tm, D_MODEL), bf16),
        lay.scratch(3 * W_BR, f32),
        pltpu.VMEM((tm, 3 * D_MODEL), f32),
        lay.scratch(3 * W_BR, bf16),
        lay.scratch(W_BR, bf16),
        lay.scratch(W_BR, f32),
        lay.slab(OFF_POOL), lay.slab(OFF_SC), lay.slab(OFF_CF),
    ]


def _mixer_prompt_kernel(h_ref, *refs, lay, n_cast, meta):
    params = refs[:N_MIX_PARAMS]
    n_src = n_cast + len(meta[0])
    srcs = refs[N_MIX_PARAMS:N_MIX_PARAMS + n_src]
    o_ref, npool_ref, nsc_ref, ncf_ref = refs[N_MIX_PARAMS + n_src:N_MIX_PARAMS + n_src + 4]
    dsts = refs[N_MIX_PARAMS + n_src + 4:N_MIX_PARAMS + n_src + n_cast + 4]
    scr = refs[N_MIX_PARAMS + n_src + n_cast + 4:]
    xp_pool, xp_sc, xp_cf = scr[-3:]
    l = lay.l
    c = pl.program_id(1)
    carried = ((xp_pool, OFF_POOL, POOL_BUF, npool_ref), (xp_sc, OFF_SC, SCONV_K - 1, nsc_ref),
               (xp_cf, OFF_CF, CCONV_K - 1, ncf_ref))

    @pl.when(c == 0)
    def _():
        for xp, off, _, _ in carried:
            xp[:, 0:off, :] = jnp.zeros((N_SLAB, off, LANES), f32)

    def write_o(val):
        o_ref[...] = val

    _mixer_body(lay, lambda: h_ref[...], write_o, c * l, *params, *scr)

    for xp, off, nb, _ in carried:
        xp[:, off - nb:off, :] = xp[:, off + l - nb:off + l, :]

    @pl.when(c == pl.num_programs(1) - 1)
    def _():
        for xp, off, nb, out_ref in carried:
            for j in range(N_SLAB):
                out_ref[0, :, j * LANES:(j + 1) * LANES] = xp[j, off - nb:off, :]

    _run_casts(srcs, dsts, meta)


def _mixer_sample_kernel(h_ref, spool_ref, ssc_ref, scf_ref, *refs, lay):
    params = refs[:N_MIX_PARAMS]
    o_ref, npool_ref, nsc_ref, ncf_ref = refs[N_MIX_PARAMS:N_MIX_PARAMS + 4]
    scr = refs[N_MIX_PARAMS + 4:]
    xp_pool, xp_sc, xp_cf = scr[-3:]
    l = lay.l
    carried = ((xp_pool, OFF_POOL, POOL_BUF, spool_ref, npool_ref), (xp_sc, OFF_SC, SCONV_K - 1, ssc_ref, nsc_ref),
               (xp_cf, OFF_CF, CCONV_K - 1, scf_ref, ncf_ref))
    for xp, off, nb, in_ref, _ in carried:
        for j in range(N_SLAB):
            xp[j, off - nb:off, :, :] = in_ref[:, :, j * LANES:(j + 1) * LANES]
    assert PAST_LEN + 1 >= max(POOL_WINDOWS)

    def write_o(val):
        o_ref[...] = val.reshape(l, lay.ns, D_MODEL)

    _mixer_body(lay, lambda: h_ref[...].reshape(lay.tm, D_MODEL), write_o, None, *params, *scr)
    for xp, off, nb, _, out_ref in carried:
        for j in range(N_SLAB):
            out_ref[:, :, j * LANES:(j + 1) * LANES] = xp[j, off + l - nb:off + l, :, :]


def _mixer_prompt_call(h, params, casts, *, batch, seq, l):
    ch = seq // l
    lay = _OneSequence(l)
    row = pl.BlockSpec((l, D_MODEL), lambda b, c: (b * ch + c, 0))
    c_in, c_out, c_shapes, c_arrays, meta = _cast_plan(casts, batch * ch, lambda b, c: b * ch + c)

    def state_spec(nb):
        return pl.BlockSpec((1, nb, W_BR), lambda b, c: (b, 0, 0))

    return pl.pallas_call(
        functools.partial(_mixer_prompt_kernel, lay=lay, n_cast=len(casts), meta=meta),
        grid=(batch, ch),
        in_specs=[row] + [_spec(a, i) for a, i in params] + c_in,
        out_specs=[row, state_spec(POOL_BUF), state_spec(SCONV_K - 1), state_spec(CCONV_K - 1)] + c_out,
        out_shape=[jax.ShapeDtypeStruct(h.shape, f32),
                   jax.ShapeDtypeStruct((batch, POOL_BUF, W_BR), f32),
                   jax.ShapeDtypeStruct((batch, SCONV_K - 1, W_BR), f32),
                   jax.ShapeDtypeStruct((batch, CCONV_K - 1, W_BR), f32)] + c_shapes,
        scratch_shapes=_mixer_scratch(lay),
        compiler_params=_params(ndim=2),
        name="mixer_prompt",
    )(h, *[a for a, _ in params], *c_arrays)


def _mixer_sample_call(h, i, spool, ssc, scf, params, *, ns):
    l, nseq, _ = h.shape
    lay = _StepMajor(l, ns)
    row = pl.BlockSpec((l, ns, D_MODEL), lambda s: (0, s, 0))

    def state_in(nb):
        return pl.BlockSpec((None, nb, ns, W_BR), lambda s: (i, 0, s, 0))

    def state_out(nb):
        return pl.BlockSpec((nb, ns, W_BR), lambda s: (0, s, 0))

    nbs = (POOL_BUF, SCONV_K - 1, CCONV_K - 1)
    return pl.pallas_call(
        functools.partial(_mixer_sample_kernel, lay=lay),
        grid=(nseq // ns,),
        in_specs=[row] + [state_in(nb) for nb in nbs] + [_spec(a, j) for a, j in params],
        out_specs=[row] + [state_out(nb) for nb in nbs],
        out_shape=[jax.ShapeDtypeStruct(h.shape, f32)] + [jax.ShapeDtypeStruct((nb, nseq, W_BR), f32) for nb in nbs],
        scratch_shapes=_mixer_scratch(lay),
        compiler_params=_params(),
        name="mixer_sample",
    )(h, spool, ssc, scf, *[a for a, _ in params])


def kernel(x_prompt, x_sample, p_prompt, p_sample, state_pool, state_sconv, state_cconv, g_f1_pre, w_f1_gate, w_f1_up, w_f1_down, g_f1_post, g_mix_pre, w_in, pool_w, pool_scale, w_pool_out, sc_conv, w_sc_out, cf_conv, cf_conv_b, cf_ln_g, cf_ln_b, w_cf_out, w_o, g_mix_post, g_f2_pre, w_f2_gate, w_f2_up, w_f2_down, g_f2_post, g_ple_pre, w_ple_gate, w_ple_proj, g_ple_post, g_final):
    depth = w_in.shape[0]
    batch, seq, _ = x_prompt.shape
    nseq, dec_seq, _ = x_sample.shape
    tm = 512
    tm_tok = 1024
    ns = 32

    def vec(a, i):
        return (a.reshape(depth, 1, a.shape[-1]), i)

    def whole(a):
        return (a, None)

    gfin = whole(g_final.reshape(1, -1))
    pool_w2 = pool_w.reshape(depth, W_BR, LANES)

    hp = x_prompt.reshape(batch * seq, D_MODEL)
    hs = x_sample.transpose(1, 0, 2).reshape(dec_seq * nseq, D_MODEL)
    pp = p_prompt.reshape(depth, batch * seq, PLE_DIM)
    ps = p_sample.transpose(0, 2, 1, 3).reshape(depth, dec_seq * nseq, PLE_DIM)
    spool, ssc, scf = (s.transpose(0, 2, 1, 3) for s in (state_pool, state_sconv, state_cconv))

    f1_w = [(w_f1_gate[0] * g_f1_pre[0][:, None]).astype(bf16), (w_f1_up[0] * g_f1_pre[0][:, None]).astype(bf16),
            w_f1_down[0].astype(bf16)]
    new_p, new_s = [], []
    for i in range(depth):
        f1 = [whole(w) for w in f1_w] + [vec(g_f1_post, i)]
        mix_src = (w_in, pool_w2, w_pool_out, w_sc_out, w_cf_out, w_o)
        hp, *mix_w = _ffn_call(hp, f1, [(w_in, i, None, IN_PERM)] + [(w, i, None) for w in mix_src[1:]], tm=tm_tok)
        hs, = _ffn_call(hs, f1, tm=tm)
        wi, pw, wpo, wso, wco, wo = mix_w
        mix = [vec(g_mix_pre, i), whole(wi), whole(pw.reshape(len(POOL_WINDOWS), LANES, LANES)), vec(pool_scale, i),
               whole(wpo), (sc_conv, i), whole(wso), (cf_conv, i), vec(cf_conv_b, i), vec(cf_ln_g, i),
               vec(cf_ln_b, i), whole(wco), whole(wo), vec(g_mix_post, i)]
        f2_src = ((w_f2_gate, i, g_f2_pre), (w_f2_up, i, g_f2_pre), (w_f2_down, i, None),
                  (w_ple_gate, i, g_ple_pre), (w_ple_proj, i, None))
        hp, *rest = _mixer_prompt_call(hp, mix, f2_src, batch=batch, seq=seq, l=tm)
        st_p, f2_w = rest[:3], rest[3:]
        hs, *st_s = _mixer_sample_call(hs.reshape(dec_seq, nseq, D_MODEL), i, spool, ssc, scf, mix, ns=ns)
        hs = hs.reshape(dec_seq * nseq, D_MODEL)
        new_p.append(st_p)
        new_s.append(st_s)
        wg, wu, wd, wpg, wpp = f2_w
        f2 = [whole(wg), whole(wu), whole(wd), vec(g_f2_post, i), whole(wpg), whole(wpp), vec(g_ple_post, i), gfin]
        final = i == depth - 1
        nxt = [] if final else [(w_f1_gate, i + 1, g_f1_pre), (w_f1_up, i + 1, g_f1_pre), (w_f1_down, i + 1, None)]
        hp, *f1_w = _ffn_ple_call(hp, pp, i, f2, nxt, tm=tm_tok, final=final)
        hs, = _ffn_ple_call(hs, ps, i, f2, tm=tm, final=final)
    y_sample = hs.reshape(dec_seq, nseq, D_MODEL).transpose(1, 0, 2)
    stack_p = lambda k: jnp.stack([s[k] for s in new_p])
    stack_s = lambda k: jnp.stack([s[k] for s in new_s]).transpose(0, 2, 1, 3)
    return (hp.reshape(batch, seq, D_MODEL), y_sample,
            stack_p(0), stack_p(1), stack_p(2), stack_s(0), stack_s(1), stack_s(2))
```

```python
import functools

import jax
import jax.numpy as jnp
from jax import lax
from jax.experimental import pallas as pl
from jax.experimental.pallas import tpu as pltpu

D_MODEL = 1024
D_FF = 2816
PLE_DIM = 256
W_BR = 512
POOL_WINDOWS = (2, 4, 8, 16)
POOL_BUF = 15
SCONV_K = 3
CCONV_K = 31
PAST_LEN = 16384
EPS = 1e-6

LANES = 128
SUBLANES = 8
BF16_ROWS = 16
N_SLAB = W_BR // LANES
CONV_ROWS = 64

C_A, C_V, C_BG, C_CG, C_GA, C_GG, C_G0 = 0, 512, 1024, 1536, 2048, 2560, 3072
IN_COLS = C_G0 + 3 * D_MODEL
G_PART = 3 * D_MODEL // N_SLAB
C_F = C_GA
F_WIDTH = 2 * LANES + G_PART
IN_PERM = ((0, 0, C_GA),) + tuple(
    x for j in range(N_SLAB) for x in (
        (C_F + j * F_WIDTH, C_GA + j * LANES, LANES),
        (C_F + j * F_WIDTH + LANES, C_GG + j * LANES, LANES),
        (C_F + j * F_WIDTH + 2 * LANES, C_G0 + j * G_PART, G_PART)))

FF_CHUNKS = ((0, 1024), (1024, 1024), (2048, 768))
MIX_CHUNK = 512

VMEM_LIMIT = 56 * 1024 * 1024

f32 = jnp.float32
bf16 = jnp.bfloat16


def _round_up(x, m):
    return (x + m - 1) // m * m


def _rms(x, g):
    return x * lax.rsqrt(jnp.mean(x * x, axis=-1, keepdims=True) + EPS) * g


def _dot(a, b):
    return jnp.dot(a, b, preferred_element_type=f32)


def _resident(shape):
    nd = len(shape)
    return pl.BlockSpec(shape, lambda *_: (0,) * nd, pipeline_mode=pl.Buffered(1))


def _layer(shape, i):
    nd = len(shape)
    return pl.BlockSpec((None,) + tuple(shape[1:]), lambda *_: (i,) + (0,) * (nd - 1), pipeline_mode=pl.Buffered(1))


def _spec(a, i):
    return _resident(a.shape) if i is None else _layer(a.shape, i)


def _params(ndim=1):
    return pltpu.CompilerParams(dimension_semantics=("arbitrary",) * ndim, vmem_limit_bytes=VMEM_LIMIT)


def _cast_plan(casts, nsteps, lin):
    ins, gains, outs, shapes, arrays, gain_arrays, scaled, perms = [], [], [], [], [], [], [], {}
    for k, (src, i, gain, *perm) in enumerate(casts):
        if perm:
            perms[k] = perm[0]
        _, r, c = src.shape
        div = 1
        while (r * div) % nsteps or (r * div // nsteps) % BF16_ROWS:
            div *= 2
        br = r * div // nsteps
        ins.append(pl.BlockSpec((None, br, c), lambda *g, i=i, div=div: (i, lin(*g) // div, 0)))
        outs.append(pl.BlockSpec((br, c), lambda *g, div=div: (lin(*g) // div, 0)))
        shapes.append(jax.ShapeDtypeStruct((r, c), bf16))
        arrays.append(src)
        if gain is not None:
            gains.append(pl.BlockSpec((None, br, 1), lambda *g, i=i, div=div: (i, lin(*g) // div, 0)))
            gain_arrays.append(gain.reshape(gain.shape[0], r, 1))
            scaled.append(k)
    return ins + gains, outs, shapes, arrays + gain_arrays, (tuple(scaled), tuple(perms.items()))


def _run_casts(srcs, dsts, meta):
    scaled, perms = meta
    gains = dict(zip(scaled, srcs[len(dsts):]))
    perms = dict(perms)
    for k, d in enumerate(dsts):
        for dst, src, width in perms.get(k, ((0, 0, d.shape[-1]),)):
            w = srcs[k][:, src:src + width]
            if k in gains:
                w = w * gains[k][...]
            d[:, dst:dst + width] = w.astype(bf16)


def _row_scale(x):
    return lax.rsqrt(jnp.mean(x * x, axis=-1, keepdims=True) + EPS)


def _swiglu_into(acc_scr, u_scr, rs, wg_ref, wu_ref, wd_ref, wait):
    for n, (c0, cw) in enumerate(FF_CHUNKS):
        wait(n)
        u = u_scr[...]
        g = _dot(u, wg_ref[:, c0:c0 + cw]) * rs
        up = _dot(u, wu_ref[:, c0:c0 + cw]) * rs
        a = (jax.nn.silu(g) * up).astype(bf16)
        d = _dot(a, wd_ref[c0:c0 + cw, :])
        if n == 0:
            acc_scr[...] = d
        else:
            acc_scr[...] += d


def _split_refs(refs, n_in, n_out, meta):
    n_cast_in = n_out + len(meta[0])
    return (refs[:n_in], refs[n_in:n_in + n_cast_in], refs[n_in + n_cast_in],
            refs[n_in + n_cast_in + 1:n_in + n_cast_in + 1 + n_out], refs[n_in + n_cast_in + 1 + n_out:])


def _ffn_stages(sem, wg, wu, wd, *whole):
    stages, k = [], 0
    for c0, cw in FF_CHUNKS:
        stage = []
        for (src, dst), sl in ((wg, (slice(None), pl.ds(c0, cw))), (wu, (slice(None), pl.ds(c0, cw))),
                               (wd, (pl.ds(c0, cw), slice(None)))):
            stage.append(pltpu.make_async_copy(src.at[sl], dst.at[sl], sem.at[k]))
            k += 1
        stages.append(stage)
    if whole:
        stages.append([pltpu.make_async_copy(src, dst, sem.at[k + n]) for n, (src, dst) in enumerate(whole)])
    return stages


def _run_tile(stages, tile):
    if stages is None:
        tile(lambda k: None)
        return
    first = pl.program_id(0) == 0

    @pl.when(first)
    def _():
        for stage in stages:
            for cp in stage:
                cp.start()

        def wait(k):
            for cp in stages[k]:
                cp.wait()

        tile(wait)

    @pl.when(jnp.logical_not(first))
    def _():
        tile(lambda k: None)


def _ffn_kernel(*refs, n_cast, meta):
    (h_ref, wg_ref, wu_ref, wd_ref, gpost_ref), srcs, o_ref, dsts, (u_scr, acc_scr, *staging) = _split_refs(
        refs, 5, n_cast, meta)
    stages = None
    if staging:
        (wg_hbm, wu_hbm, wd_hbm), (wg_ref, wu_ref, wd_ref, sem) = (wg_ref, wu_ref, wd_ref), staging
        stages = _ffn_stages(sem, (wg_hbm, wg_ref), (wu_hbm, wu_ref), (wd_hbm, wd_ref))

    def tile(wait):
        h = h_ref[...]
        u_scr[...] = h.astype(bf16)
        _swiglu_into(acc_scr, u_scr, _row_scale(h), wg_ref, wu_ref, wd_ref, wait)
        o_ref[...] = h_ref[...] + _rms(acc_scr[...], 0.5 * gpost_ref[...])

    _run_tile(stages, tile)
    _run_casts(srcs, dsts, meta)


def _ffn_ple_kernel(*refs, n_cast, meta, final):
    ((h_ref, p_ref, wg_ref, wu_ref, wd_ref, gpost_ref, wpg_ref, wpp_ref, gppost_ref, gfin_ref), srcs, o_ref, dsts,
     (u_scr, acc_scr, *staging)) = _split_refs(refs, 10, n_cast, meta)
    stages = None
    if staging:
        hbm = (wg_ref, wu_ref, wd_ref, wpg_ref, wpp_ref)
        wg_ref, wu_ref, wd_ref, wpg_ref, wpp_ref, sem = staging
        stages = _ffn_stages(sem, *zip(hbm, (wg_ref, wu_ref, wd_ref, wpg_ref, wpp_ref)))

    def tile(wait):
        h = h_ref[...]
        u_scr[...] = h.astype(bf16)
        _swiglu_into(acc_scr, u_scr, _row_scale(h), wg_ref, wu_ref, wd_ref, wait)
        h = h_ref[...] + _rms(acc_scr[...], 0.5 * gpost_ref[...])
        wait(len(FF_CHUNKS))
        gate = jax.nn.sigmoid(_dot(h.astype(bf16), wpg_ref[...]) * _row_scale(h))
        proj = _dot(p_ref[...].astype(bf16), wpp_ref[...])
        h = h + _rms(proj * gate, gppost_ref[...])
        if final:
            h = _rms(h, gfin_ref[...])
        o_ref[...] = h

    _run_tile(stages, tile)
    _run_casts(srcs, dsts, meta)


STAGED = "staged"


def _token_call(kern, name, h, extra, extra_specs, params, casts, *, tm):
    steps = h.shape[0] // tm
    row = pl.BlockSpec((tm, D_MODEL), lambda r: (r, 0))
    c_in, c_out, c_shapes, c_arrays, meta = _cast_plan(casts, steps, lambda r: r)
    staged = [a for a, i in params if i is STAGED]
    staging = [pltpu.VMEM(a.shape, a.dtype) for a in staged]
    if staged:
        staging.append(pltpu.SemaphoreType.DMA((3 * len(FF_CHUNKS) + len(staged) - 3,)))
    return pl.pallas_call(
        functools.partial(kern, n_cast=len(casts), meta=meta),
        grid=(steps,),
        in_specs=[row] + extra_specs + [pl.BlockSpec(memory_space=pl.ANY) if i is STAGED else _spec(a, i)
                                        for a, i in params] + c_in,
        out_specs=[row] + c_out,
        out_shape=[jax.ShapeDtypeStruct(h.shape, f32)] + c_shapes,
        scratch_shapes=[pltpu.VMEM((tm, D_MODEL), bf16), pltpu.VMEM((tm, D_MODEL), f32)] + staging,
        compiler_params=_params(),
        name=name,
    )(h, *extra, *[a for a, _ in params], *c_arrays)


def _ffn_call(h, params, casts=(), *, tm):
    return _token_call(_ffn_kernel, "ffn", h, [], [], params, casts, tm=tm)


def _ffn_ple_call(h, p, i, params, casts=(), *, tm, final):
    prow = pl.BlockSpec((None, tm, PLE_DIM), lambda r: (i, r, 0))
    return _token_call(functools.partial(_ffn_ple_kernel, final=final), "ffn_ple", h, [p], [prow], params, casts, tm=tm)


OFF_POOL = _round_up(POOL_BUF, SUBLANES)
OFF_SC = _round_up(SCONV_K - 1, SUBLANES)
OFF_CF = _round_up(CCONV_K - 1, SUBLANES)


class _OneSequence:
    def __init__(self, l):
        self.l, self.tm = l, l
        self.blocks = tuple(range(0, l, CONV_ROWS))
        self.row_parts = tuple((r0, l // 2) for r0 in (0, l // 2))

    def blocks_in(self, r0, nr):
        return tuple(range(r0, r0 + nr, CONV_ROWS))

    def part_to_slab(self, xp, off, j, r0, nr, val):
        xp[j, off + r0:off + r0 + nr, :] = val

    def scratch(self, cols, dtype):
        return pltpu.VMEM((self.l, cols), dtype)

    def slab(self, off):
        return pltpu.VMEM((N_SLAB, off + self.l, LANES), f32)

    def load(self, ref, cs=slice(None)):
        return ref[:, cs]

    def store(self, ref, cs, val):
        ref[:, cs] = val

    def blk_load(self, ref, b, cs):
        return ref[b:b + CONV_ROWS, cs]

    def blk_store(self, ref, b, cs, val):
        ref[b:b + CONV_ROWS, cs] = val

    def win(self, xp, j, start, b):
        return xp[j, start + b:start + b + CONV_ROWS, :]

    def conv(self, xp, j, start, b, taps):
        acc = None
        for k, w in enumerate(taps):
            term = self.win(xp, j, start + k, b) * w
            acc = term if acc is None else acc + term
        return acc

    def tap(self, w_ref, k, cs):
        return w_ref[k:k + 1, cs]

    def to_slab(self, xp, off, j, val):
        xp[j, off:off + self.l, :] = val

    def row_index(self, b):
        return b + lax.broadcasted_iota(jnp.int32, (CONV_ROWS, LANES), 0)


class _StepMajor:
    def __init__(self, l, ns):
        self.l, self.ns, self.tm = l, ns, l * ns
        self.blocks = tuple(range(0, ns, BF16_ROWS))
        self.row_parts = ((0, self.tm),)

    def blocks_in(self, r0, nr):
        return self.blocks

    def part_to_slab(self, xp, off, j, r0, nr, val):
        self.to_slab(xp, off, j, val)

    def scratch(self, cols, dtype):
        return pltpu.VMEM((self.l, self.ns, cols), dtype)

    def slab(self, off):
        return pltpu.VMEM((N_SLAB, off + self.l, self.ns, LANES), f32)

    def load(self, ref, cs=slice(None)):
        v = ref[:, :, cs]
        return v.reshape(self.tm, v.shape[-1])

    def store(self, ref, cs, val):
        ref[:, :, cs] = val.reshape(self.l, self.ns, val.shape[-1])

    def blk_load(self, ref, b, cs):
        return ref[:, b:b + BF16_ROWS, cs]

    def blk_store(self, ref, b, cs, val):
        ref[:, b:b + BF16_ROWS, cs] = val

    def win(self, xp, j, start, b):
        return xp[j, start:start + self.l, b:b + BF16_ROWS, :]

    def conv(self, xp, j, start, b, taps):
        acc = None
        for k, w in enumerate(taps):
            term = self.win(xp, j, start + k, b) * w
            acc = term if acc is None else acc + term
        return acc

    def tap(self, w_ref, k, cs):
        return w_ref[k:k + 1, cs].reshape(1, 1, LANES)

    def to_slab(self, xp, off, j, val):
        xp[j, off:off + self.l, :, :] = val.reshape(self.l, self.ns, LANES)


def _mixer_body(lay, read_h, write_o, pos0,
                gpre_ref, win_ref, poolw_ref, pscale_ref, wpo_ref, scw_ref, wso_ref, cfw_ref, cfb_ref,
                lng_ref, lnb_ref, wco_ref, wo_ref, gpost_ref,
                u_scr, zb_scr, g_scr, y_scr, mm_scr, cc_scr, xp_pool, xp_sc, xp_cf):
    def _project_pool_shortconv():
        u_scr[...] = _rms(read_h(), gpre_ref[...]).astype(bf16)
        lay.store(zb_scr, slice(None), _dot(u_scr[...], win_ref[:, C_V:C_V + 3 * W_BR]))
        for j in range(N_SLAB):
            v = lay.load(zb_scr, slice(j * LANES, (j + 1) * LANES))
            cg = lay.load(zb_scr, slice(2 * W_BR + j * LANES, 2 * W_BR + (j + 1) * LANES))
            lay.to_slab(xp_sc, OFF_SC, j, cg * v)
        za = _dot(u_scr[...], win_ref[:, C_A:C_A + W_BR])
        for j in range(N_SLAB):
            lay.to_slab(xp_pool, OFF_POOL, j, za[:, j * LANES:(j + 1) * LANES])

        for j in range(N_SLAB):
            cs = slice(j * LANES, (j + 1) * LANES)
            taps = [lay.tap(scw_ref, k, cs) for k in range(SCONV_K)]
            for b in lay.blocks:
                cv = lay.conv(xp_sc, j, OFF_SC - (SCONV_K - 1), b, taps)
                bg = lay.blk_load(zb_scr, b, slice(W_BR + j * LANES, W_BR + (j + 1) * LANES))
                lay.blk_store(y_scr, b, slice(W_BR + j * LANES, W_BR + (j + 1) * LANES), (bg * cv).astype(bf16))

        for j, w in enumerate(POOL_WINDOWS):
            cs = slice(j * LANES, (j + 1) * LANES)
            for b in lay.blocks:
                cur = lay.win(xp_pool, j, OFF_POOL, b)
                s = cur
                for i in range(1, w):
                    s = s + lay.win(xp_pool, j, OFF_POOL - i, b)
                if pos0 is not None and b < w - 1:
                    mean = s / jnp.minimum(w, pos0 + lay.row_index(b) + 1).astype(f32)
                else:
                    mean = s * (1.0 / w)
                lay.blk_store(mm_scr, b, cs, (mean - cur).astype(bf16))
            ya = _dot(lay.load(mm_scr, cs), poolw_ref[j]) * pscale_ref[:, cs]
            lay.store(y_scr, cs, ya.astype(bf16))

    def _gates_and_conformer_conv():
        for j in range(N_SLAB):
            cs = slice(j * LANES, (j + 1) * LANES)
            taps = [lay.tap(cfw_ref, k, cs) for k in range(CCONV_K)]
            bias = cfb_ref[:, cs]
            for r0, nr in lay.row_parts:
                z = _dot(u_scr[r0:r0 + nr, :], win_ref[:, C_F + j * F_WIDTH:C_F + (j + 1) * F_WIDTH])
                glu = z[:, 0:LANES] * jax.nn.sigmoid(z[:, LANES:2 * LANES])
                lay.part_to_slab(xp_cf, OFF_CF, j, r0, nr, glu)
                g_scr[r0:r0 + nr, j * G_PART:(j + 1) * G_PART] = jax.nn.sigmoid(z[:, 2 * LANES:])
                for b in lay.blocks_in(r0, nr):
                    cc = lay.conv(xp_cf, j, OFF_CF - (CCONV_K - 1), b, taps)
                    lay.blk_store(cc_scr, b, cs, cc + bias)
        x = lay.load(cc_scr)
        xc = x - jnp.mean(x, axis=-1, keepdims=True)
        ln = xc * lax.rsqrt(jnp.mean(xc * xc, axis=-1, keepdims=True) + EPS) * lng_ref[...] + lnb_ref[...]
        lay.store(y_scr, slice(2 * W_BR, 3 * W_BR), jax.nn.silu(ln).astype(bf16))

    def _combine_and_project():
        m = []
        for c0 in range(0, D_MODEL, MIX_CHUNK):
            cs = slice(c0, c0 + MIX_CHUNK)
            acc = None
            for b, w_ref in enumerate((wpo_ref, wso_ref, wco_ref)):
                yb = lay.load(y_scr, slice(b * W_BR, (b + 1) * W_BR))
                term = g_scr[:, b * D_MODEL + c0:b * D_MODEL + c0 + MIX_CHUNK] * _dot(yb, w_ref[:, cs])
                acc = term if acc is None else acc + term
            m.append(acc.astype(bf16))
        mo = _dot(jnp.concatenate(m, axis=-1), wo_ref[...])
        write_o(read_h() + _rms(mo, gpost_ref[...]))

    _project_pool_shortconv()
    _gates_and_conformer_conv()
    _combine_and_project()


N_MIX_PARAMS = 14


def _mixer_scratch(lay):
    tm = lay.tm
    return [
        pltpu.VMEM((tm, D_MODEL), bf16),
        lay.scratch(3 * W_BR, f32),
        pltpu.VMEM((tm, 3 * D_MODEL), f32),
        lay.scratch(3 * W_BR, bf16),
        lay.scratch(W_BR, bf16),
        lay.scratch(W_BR, f32),
        lay.slab(OFF_POOL), lay.slab(OFF_SC), lay.slab(OFF_CF),
    ]


def _mixer_prompt_kernel(h_ref, *refs, lay, n_cast, meta):
    params = refs[:N_MIX_PARAMS]
    n_src = n_cast + len(meta[0])
    srcs = refs[N_MIX_PARAMS:N_MIX_PARAMS + n_src]
    o_ref, npool_ref, nsc_ref, ncf_ref = refs[N_MIX_PARAMS + n_src:N_MIX_PARAMS + n_src + 4]
    dsts = refs[N_MIX_PARAMS + n_src + 4:N_MIX_PARAMS + n_src + n_cast + 4]
    scr = refs[N_MIX_PARAMS + n_src + n_cast + 4:]
    xp_pool, xp_sc, xp_cf = scr[-3:]
    l = lay.l
    c = pl.program_id(1)
    carried = ((xp_pool, OFF_POOL, POOL_BUF, npool_ref), (xp_sc, OFF_SC, SCONV_K - 1, nsc_ref),
               (xp_cf, OFF_CF, CCONV_K - 1, ncf_ref))

    @pl.when(c == 0)
    def _():
        for xp, off, _, _ in carried:
            xp[:, 0:off, :] = jnp.zeros((N_SLAB, off, LANES), f32)

    def write_o(val):
        o_ref[...] = val

    _mixer_body(lay, lambda: h_ref[...], write_o, c * l, *params, *scr)

    for xp, off, nb, _ in carried:
        xp[:, off - nb:off, :] = xp[:, off + l - nb:off + l, :]

    @pl.when(c == pl.num_programs(1) - 1)
    def _():
        for xp, off, nb, out_ref in carried:
            for j in range(N_SLAB):
                out_ref[0, :, j * LANES:(j + 1) * LANES] = xp[j, off - nb:off, :]

    _run_casts(srcs, dsts, meta)


def _mixer_sample_kernel(h_ref, spool_ref, ssc_ref, scf_ref, *refs, lay):
    params = refs[:N_MIX_PARAMS]
    o_ref, npool_ref, nsc_ref, ncf_ref = refs[N_MIX_PARAMS:N_MIX_PARAMS + 4]
    scr = refs[N_MIX_PARAMS + 4:]
    xp_pool, xp_sc, xp_cf = scr[-3:]
    l = lay.l
    carried = ((xp_pool, OFF_POOL, POOL_BUF, spool_ref, npool_ref), (xp_sc, OFF_SC, SCONV_K - 1, ssc_ref, nsc_ref),
               (xp_cf, OFF_CF, CCONV_K - 1, scf_ref, ncf_ref))
    for xp, off, nb, in_ref, _ in carried:
        for j in range(N_SLAB):
            xp[j, off - nb:off, :, :] = in_ref[:, :, j * LANES:(j + 1) * LANES]
    assert PAST_LEN + 1 >= max(POOL_WINDOWS)

    def write_o(val):
        o_ref[...] = val.reshape(l, lay.ns, D_MODEL)

    _mixer_body(lay, lambda: h_ref[...].reshape(lay.tm, D_MODEL), write_o, None, *params, *scr)
    for xp, off, nb, _, out_ref in carried:
        for j in range(N_SLAB):
            out_ref[:, :, j * LANES:(j + 1) * LANES] = xp[j, off + l - nb:off + l, :, :]


def _mixer_prompt_call(h, params, casts, *, batch, seq, l):
    ch = seq // l
    lay = _OneSequence(l)
    row = pl.BlockSpec((l, D_MODEL), lambda b, c: (b * ch + c, 0))
    c_in, c_out, c_shapes, c_arrays, meta = _cast_plan(casts, batch * ch, lambda b, c: b * ch + c)

    def state_spec(nb):
        return pl.BlockSpec((1, nb, W_BR), lambda b, c: (b, 0, 0))

    return pl.pallas_call(
        functools.partial(_mixer_prompt_kernel, lay=lay, n_cast=len(casts), meta=meta),
        grid=(batch, ch),
        in_specs=[row] + [_spec(a, i) for a, i in params] + c_in,
        out_specs=[row, state_spec(POOL_BUF), state_spec(SCONV_K - 1), state_spec(CCONV_K - 1)] + c_out,
        out_shape=[jax.ShapeDtypeStruct(h.shape, f32),
                   jax.ShapeDtypeStruct((batch, POOL_BUF, W_BR), f32),
                   jax.ShapeDtypeStruct((batch, SCONV_K - 1, W_BR), f32),
                   jax.ShapeDtypeStruct((batch, CCONV_K - 1, W_BR), f32)] + c_shapes,
        scratch_shapes=_mixer_scratch(lay),
        compiler_params=_params(ndim=2),
        name="mixer_prompt",
    )(h, *[a for a, _ in params], *c_arrays)


def _mixer_sample_call(h, i, spool, ssc, scf, params, *, ns):
    l, nseq, _ = h.shape
    lay = _StepMajor(l, ns)
    row = pl.BlockSpec((l, ns, D_MODEL), lambda s: (0, s, 0))

    def state_in(nb):
        return pl.BlockSpec((None, nb, ns, W_BR), lambda s: (i, 0, s, 0))

    def state_out(nb):
        return pl.BlockSpec((nb, ns, W_BR), lambda s: (0, s, 0))

    nbs = (POOL_BUF, SCONV_K - 1, CCONV_K - 1)
    return pl.pallas_call(
        functools.partial(_mixer_sample_kernel, lay=lay),
        grid=(nseq // ns,),
        in_specs=[row] + [state_in(nb) for nb in nbs] + [_spec(a, j) for a, j in params],
        out_specs=[row] + [state_out(nb) for nb in nbs],
        out_shape=[jax.ShapeDtypeStruct(h.shape, f32)] + [jax.ShapeDtypeStruct((nb, nseq, W_BR), f32) for nb in nbs],
        scratch_shapes=_mixer_scratch(lay),
        compiler_params=_params(),
        name="mixer_sample",
    )(h, spool, ssc, scf, *[a for a, _ in params])


def kernel(x_prompt, x_sample, p_prompt, p_sample, state_pool, state_sconv, state_cconv, g_f1_pre, w_f1_gate, w_f1_up, w_f1_down, g_f1_post, g_mix_pre, w_in, pool_w, pool_scale, w_pool_out, sc_conv, w_sc_out, cf_conv, cf_conv_b, cf_ln_g, cf_ln_b, w_cf_out, w_o, g_mix_post, g_f2_pre, w_f2_gate, w_f2_up, w_f2_down, g_f2_post, g_ple_pre, w_ple_gate, w_ple_proj, g_ple_post, g_final):
    depth = w_in.shape[0]
    batch, seq, _ = x_prompt.shape
    nseq, dec_seq, _ = x_sample.shape
    tm = 512
    tm_tok = 1024
    ns = 32

    def vec(a, i):
        return (a.reshape(depth, 1, a.shape[-1]), i)

    def whole(a):
        return (a, None)

    gfin = whole(g_final.reshape(1, -1))
    pool_w2 = pool_w.reshape(depth, W_BR, LANES)

    hp = x_prompt.reshape(batch * seq, D_MODEL)
    hs = x_sample.transpose(1, 0, 2).reshape(dec_seq * nseq, D_MODEL)
    pp = p_prompt.reshape(depth, batch * seq, PLE_DIM)
    ps = p_sample.transpose(0, 2, 1, 3).reshape(depth, dec_seq * nseq, PLE_DIM)
    spool, ssc, scf = (s.transpose(0, 2, 1, 3) for s in (state_pool, state_sconv, state_cconv))

    f1_w = [(w_f1_gate[0] * g_f1_pre[0][:, None]).astype(bf16), (w_f1_up[0] * g_f1_pre[0][:, None]).astype(bf16),
            w_f1_down[0].astype(bf16)]
    new_p, new_s = [], []
    for i in range(depth):
        f1 = [(w, STAGED) for w in f1_w] + [vec(g_f1_post, i)]
        mix_src = (w_in, pool_w2, w_pool_out, w_sc_out, w_cf_out, w_o)
        hp, *mix_w = _ffn_call(hp, f1, [(w_in, i, None, IN_PERM)] + [(w, i, None) for w in mix_src[1:]], tm=tm_tok)
        hs, = _ffn_call(hs, f1, tm=tm)
        wi, pw, wpo, wso, wco, wo = mix_w
        mix = [vec(g_mix_pre, i), whole(wi), whole(pw.reshape(len(POOL_WINDOWS), LANES, LANES)), vec(pool_scale, i),
               whole(wpo), (sc_conv, i), whole(wso), (cf_conv, i), vec(cf_conv_b, i), vec(cf_ln_g, i),
               vec(cf_ln_b, i), whole(wco), whole(wo), vec(g_mix_post, i)]
        f2_src = ((w_f2_gate, i, g_f2_pre), (w_f2_up, i, g_f2_pre), (w_f2_down, i, None),
                  (w_ple_gate, i, g_ple_pre), (w_ple_proj, i, None))
        hp, *rest = _mixer_prompt_call(hp, mix, f2_src, batch=batch, seq=seq, l=tm)
        st_p, f2_w = rest[:3], rest[3:]
        hs, *st_s = _mixer_sample_call(hs.reshape(dec_seq, nseq, D_MODEL), i, spool, ssc, scf, mix, ns=ns)
        hs = hs.reshape(dec_seq * nseq, D_MODEL)
        new_p.append(st_p)
        new_s.append(st_s)
        wg, wu, wd, wpg, wpp = f2_w
        f2 = lambda how: [(wg, how), (wu, how), (wd, how), vec(g_f2_post, i), (wpg, how), (wpp, how),
                          vec(g_ple_post, i), gfin]
        final = i == depth - 1
        nxt = [] if final else [(w_f1_gate, i + 1, g_f1_pre), (w_f1_up, i + 1, g_f1_pre), (w_f1_down, i + 1, None)]
        hp, *f1_w = _ffn_ple_call(hp, pp, i, f2(None), nxt, tm=tm_tok, final=final)
        hs, = _ffn_ple_call(hs, ps, i, f2(STAGED), tm=tm, final=final)
    y_sample = hs.reshape(dec_seq, nseq, D_MODEL).transpose(1, 0, 2)
    stack_p = lambda k: jnp.stack([s[k] for s in new_p])
    stack_s = lambda k: jnp.stack([s[k] for s in new_s]).transpose(0, 2, 1, 3)
    return (hp.reshape(batch, seq, D_MODEL), y_sample,
            stack_p(0), stack_p(1), stack_p(2), stack_s(0), stack_s(1), stack_s(2))
```

```python
import functools

import jax
import jax.numpy as jnp
from jax import lax
from jax.experimental import pallas as pl
from jax.experimental.pallas import tpu as pltpu

D_MODEL = 1024
D_FF = 2816
PLE_DIM = 256
W_BR = 512
POOL_WINDOWS = (2, 4, 8, 16)
POOL_BUF = 15
SCONV_K = 3
CCONV_K = 31
PAST_LEN = 16384
EPS = 1e-6

LANES = 128
SUBLANES = 8
BF16_ROWS = 16
N_SLAB = W_BR // LANES
CONV_ROWS = 64

C_A, C_V, C_BG, C_CG, C_GA, C_GG, C_G0 = 0, 512, 1024, 1536, 2048, 2560, 3072
IN_COLS = C_G0 + 3 * D_MODEL
G_PART = 3 * D_MODEL // N_SLAB
C_F = C_GA
F_WIDTH = 2 * LANES + G_PART
IN_PERM = ((0, 0, C_GA),) + tuple(
    x for j in range(N_SLAB) for x in (
        (C_F + j * F_WIDTH, C_GA + j * LANES, LANES),
        (C_F + j * F_WIDTH + LANES, C_GG + j * LANES, LANES),
        (C_F + j * F_WIDTH + 2 * LANES, C_G0 + j * G_PART, G_PART)))

FF_CHUNKS = ((0, 1024), (1024, 1024), (2048, 768))
MIX_CHUNK = 512

VMEM_LIMIT = 56 * 1024 * 1024

f32 = jnp.float32
bf16 = jnp.bfloat16


def _round_up(x, m):
    return (x + m - 1) // m * m


def _rms(x, g):
    return x * lax.rsqrt(jnp.mean(x * x, axis=-1, keepdims=True) + EPS) * g


def _dot(a, b):
    return jnp.dot(a, b, preferred_element_type=f32)


def _resident(shape):
    nd = len(shape)
    return pl.BlockSpec(shape, lambda *_: (0,) * nd, pipeline_mode=pl.Buffered(1))


def _layer(shape, i):
    nd = len(shape)
    return pl.BlockSpec((None,) + tuple(shape[1:]), lambda *_: (i,) + (0,) * (nd - 1), pipeline_mode=pl.Buffered(1))


def _spec(a, i):
    return _resident(a.shape) if i is None else _layer(a.shape, i)


def _params(ndim=1):
    return pltpu.CompilerParams(dimension_semantics=("arbitrary",) * ndim, vmem_limit_bytes=VMEM_LIMIT)


def _cast_plan(casts, nsteps, lin):
    ins, gains, outs, shapes, arrays, gain_arrays, scaled, perms = [], [], [], [], [], [], [], {}
    for k, (src, i, gain, *perm) in enumerate(casts):
        if perm:
            perms[k] = perm[0]
        _, r, c = src.shape
        div = 1
        while (r * div) % nsteps or (r * div // nsteps) % BF16_ROWS:
            div *= 2
        br = r * div // nsteps
        ins.append(pl.BlockSpec((None, br, c), lambda *g, i=i, div=div: (i, lin(*g) // div, 0)))
        outs.append(pl.BlockSpec((br, c), lambda *g, div=div: (lin(*g) // div, 0)))
        shapes.append(jax.ShapeDtypeStruct((r, c), bf16))
        arrays.append(src)
        if gain is not None:
            gains.append(pl.BlockSpec((None, br, 1), lambda *g, i=i, div=div: (i, lin(*g) // div, 0)))
            gain_arrays.append(gain.reshape(gain.shape[0], r, 1))
            scaled.append(k)
    return ins + gains, outs, shapes, arrays + gain_arrays, (tuple(scaled), tuple(perms.items()))


def _run_casts(srcs, dsts, meta):
    scaled, perms = meta
    gains = dict(zip(scaled, srcs[len(dsts):]))
    perms = dict(perms)
    for k, d in enumerate(dsts):
        for dst, src, width in perms.get(k, ((0, 0, d.shape[-1]),)):
            w = srcs[k][:, src:src + width]
            if k in gains:
                w = w * gains[k][...]
            d[:, dst:dst + width] = w.astype(bf16)


def _row_scale(x):
    return lax.rsqrt(jnp.mean(x * x, axis=-1, keepdims=True) + EPS)


def _swiglu_into(acc_scr, u_scr, rs, wg_ref, wu_ref, wd_ref):
    for n, (c0, cw) in enumerate(FF_CHUNKS):
        u = u_scr[...]
        g = _dot(u, wg_ref[:, c0:c0 + cw]) * rs
        up = _dot(u, wu_ref[:, c0:c0 + cw]) * rs
        a = (jax.nn.silu(g) * up).astype(bf16)
        d = _dot(a, wd_ref[c0:c0 + cw, :])
        if n == 0:
            acc_scr[...] = d
        else:
            acc_scr[...] += d


def _split_refs(refs, n_in, n_out, meta):
    n_cast_in = n_out + len(meta[0])
    return (refs[:n_in], refs[n_in:n_in + n_cast_in], refs[n_in + n_cast_in],
            refs[n_in + n_cast_in + 1:n_in + n_cast_in + 1 + n_out], refs[n_in + n_cast_in + 1 + n_out:])


def _ffn_kernel(*refs, n_cast, meta):
    (h_ref, wg_ref, wu_ref, wd_ref, gpost_ref), srcs, o_ref, dsts, (u_scr, acc_scr) = _split_refs(refs, 5, n_cast, meta)
    h = h_ref[...]
    u_scr[...] = h.astype(bf16)
    _swiglu_into(acc_scr, u_scr, _row_scale(h), wg_ref, wu_ref, wd_ref)
    o_ref[...] = h_ref[...] + _rms(acc_scr[...], 0.5 * gpost_ref[...])
    _run_casts(srcs, dsts, meta)


def _ffn_ple_kernel(*refs, n_cast, meta, final):
    ((h_ref, p_ref, wg_ref, wu_ref, wd_ref, gpost_ref, wpg_ref, wpp_ref, gppost_ref, gfin_ref),
     srcs, o_ref, dsts, (u_scr, acc_scr)) = _split_refs(refs, 10, n_cast, meta)
    h = h_ref[...]
    u_scr[...] = h.astype(bf16)
    _swiglu_into(acc_scr, u_scr, _row_scale(h), wg_ref, wu_ref, wd_ref)
    h = h_ref[...] + _rms(acc_scr[...], 0.5 * gpost_ref[...])
    gate = jax.nn.sigmoid(_dot(h.astype(bf16), wpg_ref[...]) * _row_scale(h))
    proj = _dot(p_ref[...].astype(bf16), wpp_ref[...])
    h = h + _rms(proj * gate, gppost_ref[...])
    if final:
        h = _rms(h, gfin_ref[...])
    o_ref[...] = h
    _run_casts(srcs, dsts, meta)


def _token_call(kern, name, h, extra, extra_specs, params, casts, *, tm):
    steps = h.shape[0] // tm
    row = pl.BlockSpec((tm, D_MODEL), lambda r: (r, 0))
    c_in, c_out, c_shapes, c_arrays, meta = _cast_plan(casts, steps, lambda r: r)
    return pl.pallas_call(
        functools.partial(kern, n_cast=len(casts), meta=meta),
        grid=(steps,),
        in_specs=[row] + extra_specs + [_spec(a, i) for a, i in params] + c_in,
        out_specs=[row] + c_out,
        out_shape=[jax.ShapeDtypeStruct(h.shape, f32)] + c_shapes,
        scratch_shapes=[pltpu.VMEM((tm, D_MODEL), bf16), pltpu.VMEM((tm, D_MODEL), f32)],
        compiler_params=_params(),
        name=name,
    )(h, *extra, *[a for a, _ in params], *c_arrays)


def _ffn_call(h, params, casts=(), *, tm):
    return _token_call(_ffn_kernel, "ffn", h, [], [], params, casts, tm=tm)


def _ffn_ple_call(h, p, i, params, casts=(), *, tm, final):
    prow = pl.BlockSpec((None, tm, PLE_DIM), lambda r: (i, r, 0))
    return _token_call(functools.partial(_ffn_ple_kernel, final=final), "ffn_ple", h, [p], [prow], params, casts, tm=tm)


OFF_POOL = _round_up(POOL_BUF, SUBLANES)
OFF_SC = _round_up(SCONV_K - 1, SUBLANES)
OFF_CF = _round_up(CCONV_K - 1, SUBLANES)


class _OneSequence:
    def __init__(self, l):
        self.l, self.tm = l, l
        self.blocks = tuple(range(0, l, CONV_ROWS))
        self.row_parts = tuple((r0, l // 2) for r0 in (0, l // 2))

    def blocks_in(self, r0, nr):
        return tuple(range(r0, r0 + nr, CONV_ROWS))

    def part_to_slab(self, xp, off, j, r0, nr, val):
        xp[j, off + r0:off + r0 + nr, :] = val

    def scratch(self, cols, dtype):
        return pltpu.VMEM((self.l, cols), dtype)

    def slab(self, off):
        return pltpu.VMEM((N_SLAB, off + self.l, LANES), f32)

    def load(self, ref, cs=slice(None)):
        return ref[:, cs]

    def store(self, ref, cs, val):
        ref[:, cs] = val

    def blk_load(self, ref, b, cs):
        return ref[b:b + CONV_ROWS, cs]

    def blk_store(self, ref, b, cs, val):
        ref[b:b + CONV_ROWS, cs] = val

    def win(self, xp, j, start, b):
        return xp[j, start + b:start + b + CONV_ROWS, :]

    def conv(self, xp, j, start, b, taps):
        acc = None
        for k, w in enumerate(taps):
            term = self.win(xp, j, start + k, b) * w
            acc = term if acc is None else acc + term
        return acc

    def tap(self, w_ref, k, cs):
        return w_ref[k:k + 1, cs]

    def to_slab(self, xp, off, j, val):
        xp[j, off:off + self.l, :] = val

    def row_index(self, b):
        return b + lax.broadcasted_iota(jnp.int32, (CONV_ROWS, LANES), 0)


class _StepMajor:
    def __init__(self, l, ns):
        self.l, self.ns, self.tm = l, ns, l * ns
        self.blocks = tuple(range(0, ns, BF16_ROWS))
        self.row_parts = ((0, self.tm),)

    def blocks_in(self, r0, nr):
        return self.blocks

    def part_to_slab(self, xp, off, j, r0, nr, val):
        self.to_slab(xp, off, j, val)

    def scratch(self, cols, dtype):
        return pltpu.VMEM((self.l, self.ns, cols), dtype)

    def slab(self, off):
        return pltpu.VMEM((N_SLAB, off + self.l, self.ns, LANES), f32)

    def load(self, ref, cs=slice(None)):
        v = ref[:, :, cs]
        return v.reshape(self.tm, v.shape[-1])

    def store(self, ref, cs, val):
        ref[:, :, cs] = val.reshape(self.l, self.ns, val.shape[-1])

    def blk_load(self, ref, b, cs):
        return ref[:, b:b + BF16_ROWS, cs]

    def blk_store(self, ref, b, cs, val):
        ref[:, b:b + BF16_ROWS, cs] = val

    def win(self, xp, j, start, b):
        return xp[j, start:start + self.l, b:b + BF16_ROWS, :]

    def conv(self, xp, j, start, b, taps):
        acc = None
        for k, w in enumerate(taps):
            term = self.win(xp, j, start + k, b) * w
            acc = term if acc is None else acc + term
        return acc

    def tap(self, w_ref, k, cs):
        return w_ref[k:k + 1, cs].reshape(1, 1, LANES)

    def to_slab(self, xp, off, j, val):
        xp[j, off:off + self.l, :, :] = val.reshape(self.l, self.ns, LANES)


def _mixer_body(lay, read_h, write_o, pos0,
                gpre_ref, win_ref, poolw_ref, pscale_ref, wpo_ref, scw_ref, wso_ref, cfw_ref, cfb_ref,
                lng_ref, lnb_ref, wco_ref, wo_ref, gpost_ref,
                u_scr, zb_scr, g_scr, y_scr, mm_scr, cc_scr, xp_pool, xp_sc, xp_cf):
    def _project_pool_shortconv():
        u_scr[...] = _rms(read_h(), gpre_ref[...]).astype(bf16)
        lay.store(zb_scr, slice(None), _dot(u_scr[...], win_ref[:, C_V:C_V + 3 * W_BR]))
        for j in range(N_SLAB):
            v = lay.load(zb_scr, slice(j * LANES, (j + 1) * LANES))
            cg = lay.load(zb_scr, slice(2 * W_BR + j * LANES, 2 * W_BR + (j + 1) * LANES))
            lay.to_slab(xp_sc, OFF_SC, j, cg * v)
        za = _dot(u_scr[...], win_ref[:, C_A:C_A + W_BR])
        for j in range(N_SLAB):
            lay.to_slab(xp_pool, OFF_POOL, j, za[:, j * LANES:(j + 1) * LANES])

        for j in range(N_SLAB):
            cs = slice(j * LANES, (j + 1) * LANES)
            taps = [lay.tap(scw_ref, k, cs) for k in range(SCONV_K)]
            for b in lay.blocks:
                cv = lay.conv(xp_sc, j, OFF_SC - (SCONV_K - 1), b, taps)
                bg = lay.blk_load(zb_scr, b, slice(W_BR + j * LANES, W_BR + (j + 1) * LANES))
                lay.blk_store(y_scr, b, slice(W_BR + j * LANES, W_BR + (j + 1) * LANES), (bg * cv).astype(bf16))

        for j, w in enumerate(POOL_WINDOWS):
            cs = slice(j * LANES, (j + 1) * LANES)
            for b in lay.blocks:
                cur = lay.win(xp_pool, j, OFF_POOL, b)
                s = cur
                for i in range(1, w):
                    s = s + lay.win(xp_pool, j, OFF_POOL - i, b)
                if pos0 is not None and b < w - 1:
                    mean = s / jnp.minimum(w, pos0 + lay.row_index(b) + 1).astype(f32)
                else:
                    mean = s * (1.0 / w)
                lay.blk_store(mm_scr, b, cs, (mean - cur).astype(bf16))
            ya = _dot(lay.load(mm_scr, cs), poolw_ref[j]) * pscale_ref[:, cs]
            lay.store(y_scr, cs, ya.astype(bf16))

    def _gates_and_conformer_conv():
        for j in range(N_SLAB):
            cs = slice(j * LANES, (j + 1) * LANES)
            taps = [lay.tap(cfw_ref, k, cs) for k in range(CCONV_K)]
            bias = cfb_ref[:, cs]
            for r0, nr in lay.row_parts:
                z = _dot(u_scr[r0:r0 + nr, :], win_ref[:, C_F + j * F_WIDTH:C_F + (j + 1) * F_WIDTH])
                glu = z[:, 0:LANES] * jax.nn.sigmoid(z[:, LANES:2 * LANES])
                lay.part_to_slab(xp_cf, OFF_CF, j, r0, nr, glu)
                g_scr[r0:r0 + nr, j * G_PART:(j + 1) * G_PART] = jax.nn.sigmoid(z[:, 2 * LANES:])
                for b in lay.blocks_in(r0, nr):
                    cc = lay.conv(xp_cf, j, OFF_CF - (CCONV_K - 1), b, taps)
                    lay.blk_store(cc_scr, b, cs, cc + bias)
        x = lay.load(cc_scr)
        xc = x - jnp.mean(x, axis=-1, keepdims=True)
        ln = xc * lax.rsqrt(jnp.mean(xc * xc, axis=-1, keepdims=True) + EPS) * lng_ref[...] + lnb_ref[...]
        lay.store(y_scr, slice(2 * W_BR, 3 * W_BR), jax.nn.silu(ln).astype(bf16))

    def _combine_and_project():
        m = []
        for c0 in range(0, D_MODEL, MIX_CHUNK):
            cs = slice(c0, c0 + MIX_CHUNK)
            acc = None
            for b, w_ref in enumerate((wpo_ref, wso_ref, wco_ref)):
                yb = lay.load(y_scr, slice(b * W_BR, (b + 1) * W_BR))
                term = g_scr[:, b * D_MODEL + c0:b * D_MODEL + c0 + MIX_CHUNK] * _dot(yb, w_ref[:, cs])
                acc = term if acc is None else acc + term
            m.append(acc.astype(bf16))
        mo = _dot(jnp.concatenate(m, axis=-1), wo_ref[...])
        write_o(read_h() + _rms(mo, gpost_ref[...]))

    _project_pool_shortconv()
    _gates_and_conformer_conv()
    _combine_and_project()


N_MIX_PARAMS = 14


def _mixer_scratch(lay):
    tm = lay.tm
    return [
        pltpu.VMEM((tm, D_MODEL), bf16),
        lay.scratch(3 * W_BR, f32),
        pltpu.VMEM((tm, 3 * D_MODEL), f32),
        lay.scratch(3 * W_BR, bf16),
        lay.scratch(W_BR, bf16),
        lay.scratch(W_BR, f32),
        lay.slab(OFF_POOL), lay.slab(OFF_SC), lay.slab(OFF_CF),
    ]


def _mixer_prompt_kernel(h_ref, *refs, lay, n_cast, meta):
    params = refs[:N_MIX_PARAMS]
    n_src = n_cast + len(meta[0])
    srcs = refs[N_MIX_PARAMS:N_MIX_PARAMS + n_src]
    o_ref, npool_ref, nsc_ref, ncf_ref = refs[N_MIX_PARAMS + n_src:N_MIX_PARAMS + n_src + 4]
    dsts = refs[N_MIX_PARAMS + n_src + 4:N_MIX_PARAMS + n_src + n_cast + 4]
    scr = refs[N_MIX_PARAMS + n_src + n_cast + 4:]
    xp_pool, xp_sc, xp_cf = scr[-3:]
    l = lay.l
    c = pl.program_id(1)
    carried = ((xp_pool, OFF_POOL, POOL_BUF, npool_ref), (xp_sc, OFF_SC, SCONV_K - 1, nsc_ref),
               (xp_cf, OFF_CF, CCONV_K - 1, ncf_ref))

    @pl.when(c == 0)
    def _():
        for xp, off, _, _ in carried:
            xp[:, 0:off, :] = jnp.zeros((N_SLAB, off, LANES), f32)

    def write_o(val):
        o_ref[...] = val

    _mixer_body(lay, lambda: h_ref[...], write_o, c * l, *params, *scr)
    _run_casts(srcs, dsts, meta)

    for xp, off, nb, out_ref in carried:
        xp[:, off - nb:off, :] = xp[:, off + l - nb:off + l, :]
        for j in range(N_SLAB):
            out_ref[0, :, j * LANES:(j + 1) * LANES] = xp[j, off - nb:off, :]


def _mixer_sample_kernel(h_ref, spool_ref, ssc_ref, scf_ref, *refs, lay):
    params = refs[:N_MIX_PARAMS]
    o_ref, npool_ref, nsc_ref, ncf_ref = refs[N_MIX_PARAMS:N_MIX_PARAMS + 4]
    scr = refs[N_MIX_PARAMS + 4:]
    xp_pool, xp_sc, xp_cf = scr[-3:]
    l = lay.l
    carried = ((xp_pool, OFF_POOL, POOL_BUF, spool_ref, npool_ref), (xp_sc, OFF_SC, SCONV_K - 1, ssc_ref, nsc_ref),
               (xp_cf, OFF_CF, CCONV_K - 1, scf_ref, ncf_ref))
    for xp, off, nb, in_ref, _ in carried:
        for j in range(N_SLAB):
            xp[j, off - nb:off, :, :] = in_ref[:, :, j * LANES:(j + 1) * LANES]
    assert PAST_LEN + 1 >= max(POOL_WINDOWS)

    def write_o(val):
        o_ref[...] = val.reshape(l, lay.ns, D_MODEL)

    _mixer_body(lay, lambda: h_ref[...].reshape(lay.tm, D_MODEL), write_o, None, *params, *scr)
    for xp, off, nb, _, out_ref in carried:
        for j in range(N_SLAB):
            out_ref[:, :, j * LANES:(j + 1) * LANES] = xp[j, off + l - nb:off + l, :, :]


def _mixer_prompt_call(h, params, casts, *, batch, seq, l):
    ch = seq // l
    lay = _OneSequence(l)
    row = pl.BlockSpec((l, D_MODEL), lambda b, c: (b * ch + c, 0))
    c_in, c_out, c_shapes, c_arrays, meta = _cast_plan(casts, batch * ch, lambda b, c: b * ch + c)

    def state_spec(nb):
        return pl.BlockSpec((1, nb, W_BR), lambda b, c: (b, 0, 0))

    return pl.pallas_call(
        functools.partial(_mixer_prompt_kernel, lay=lay, n_cast=len(casts), meta=meta),
        grid=(batch, ch),
        in_specs=[row] + [_spec(a, i) for a, i in params] + c_in,
        out_specs=[row, state_spec(POOL_BUF), state_spec(SCONV_K - 1), state_spec(CCONV_K - 1)] + c_out,
        out_shape=[jax.ShapeDtypeStruct(h.shape, f32),
                   jax.ShapeDtypeStruct((batch, POOL_BUF, W_BR), f32),
                   jax.ShapeDtypeStruct((batch, SCONV_K - 1, W_BR), f32),
                   jax.ShapeDtypeStruct((batch, CCONV_K - 1, W_BR), f32)] + c_shapes,
        scratch_shapes=_mixer_scratch(lay),
        compiler_params=_params(ndim=2),
        name="mixer_prompt",
    )(h, *[a for a, _ in params], *c_arrays)


def _mixer_sample_call(h, i, spool, ssc, scf, params, *, ns):
    l, nseq, _ = h.shape
    lay = _StepMajor(l, ns)
    row = pl.BlockSpec((l, ns, D_MODEL), lambda s: (0, s, 0))

    def state_in(nb):
        return pl.BlockSpec((None, nb, ns, W_BR), lambda s: (i, 0, s, 0))

    def state_out(nb):
        return pl.BlockSpec((nb, ns, W_BR), lambda s: (0, s, 0))

    nbs = (POOL_BUF, SCONV_K - 1, CCONV_K - 1)
    return pl.pallas_call(
        functools.partial(_mixer_sample_kernel, lay=lay),
        grid=(nseq // ns,),
        in_specs=[row] + [state_in(nb) for nb in nbs] + [_spec(a, j) for a, j in params],
        out_specs=[row] + [state_out(nb) for nb in nbs],
        out_shape=[jax.ShapeDtypeStruct(h.shape, f32)] + [jax.ShapeDtypeStruct((nb, nseq, W_BR), f32) for nb in nbs],
        scratch_shapes=_mixer_scratch(lay),
        compiler_params=_params(),
        name="mixer_sample",
    )(h, spool, ssc, scf, *[a for a, _ in params])


def kernel(x_prompt, x_sample, p_prompt, p_sample, state_pool, state_sconv, state_cconv, g_f1_pre, w_f1_gate, w_f1_up, w_f1_down, g_f1_post, g_mix_pre, w_in, pool_w, pool_scale, w_pool_out, sc_conv, w_sc_out, cf_conv, cf_conv_b, cf_ln_g, cf_ln_b, w_cf_out, w_o, g_mix_post, g_f2_pre, w_f2_gate, w_f2_up, w_f2_down, g_f2_post, g_ple_pre, w_ple_gate, w_ple_proj, g_ple_post, g_final):
    depth = w_in.shape[0]
    batch, seq, _ = x_prompt.shape
    nseq, dec_seq, _ = x_sample.shape
    tm = 512
    tm_tok = 1024
    ns = 32

    def vec(a, i):
        return (a.reshape(depth, 1, a.shape[-1]), i)

    def whole(a):
        return (a, None)

    gfin = whole(g_final.reshape(1, -1))
    pool_w2 = pool_w.reshape(depth, W_BR, LANES)

    hp = x_prompt.reshape(batch * seq, D_MODEL)
    hs = x_sample.transpose(1, 0, 2).reshape(dec_seq * nseq, D_MODEL)
    pp = p_prompt.reshape(depth, batch * seq, PLE_DIM)
    ps = p_sample.transpose(0, 2, 1, 3).reshape(depth, dec_seq * nseq, PLE_DIM)
    spool, ssc, scf = (s.transpose(0, 2, 1, 3) for s in (state_pool, state_sconv, state_cconv))

    f1_w = [(w_f1_gate[0] * g_f1_pre[0][:, None]).astype(bf16), (w_f1_up[0] * g_f1_pre[0][:, None]).astype(bf16),
            w_f1_down[0].astype(bf16)]
    new_p, new_s = [], []
    for i in range(depth):
        f1 = [whole(w) for w in f1_w] + [vec(g_f1_post, i)]
        mix_src = (w_in, pool_w2, w_pool_out, w_sc_out, w_cf_out, w_o)
        hp, *mix_w = _ffn_call(hp, f1, [(w_in, i, None, IN_PERM)] + [(w, i, None) for w in mix_src[1:]], tm=tm_tok)
        hs, = _ffn_call(hs, f1, tm=tm)
        wi, pw, wpo, wso, wco, wo = mix_w
        mix = [vec(g_mix_pre, i), whole(wi), whole(pw.reshape(len(POOL_WINDOWS), LANES, LANES)), vec(pool_scale, i),
               whole(wpo), (sc_conv, i), whole(wso), (cf_conv, i), vec(cf_conv_b, i), vec(cf_ln_g, i),
               vec(cf_ln_b, i), whole(wco), whole(wo), vec(g_mix_post, i)]
        f2_src = ((w_f2_gate, i, g_f2_pre), (w_f2_up, i, g_f2_pre), (w_f2_down, i, None),
                  (w_ple_gate, i, g_ple_pre), (w_ple_proj, i, None))
        hp, *rest = _mixer_prompt_call(hp, mix, f2_src, batch=batch, seq=seq, l=tm)
        st_p, f2_w = rest[:3], rest[3:]
        hs, *st_s = _mixer_sample_call(hs.reshape(dec_seq, nseq, D_MODEL), i, spool, ssc, scf, mix, ns=ns)
        hs = hs.reshape(dec_seq * nseq, D_MODEL)
        new_p.append(st_p)
        new_s.append(st_s)
        wg, wu, wd, wpg, wpp = f2_w
        f2 = [whole(wg), whole(wu), whole(wd), vec(g_f2_post, i), whole(wpg), whole(wpp), vec(g_ple_post, i), gfin]
        final = i == depth - 1
        nxt = [] if final else [(w_f1_gate, i + 1, g_f1_pre), (w_f1_up, i + 1, g_f1_pre), (w_f1_down, i + 1, None)]
        hp, *f1_w = _ffn_ple_call(hp, pp, i, f2, nxt, tm=tm_tok, final=final)
        hs, = _ffn_ple_call(hs, ps, i, f2, tm=tm, final=final)
    y_sample = hs.reshape(dec_seq, nseq, D_MODEL).transpose(1, 0, 2)
    stack_p = lambda k: jnp.stack([s[k] for s in new_p])
    stack_s = lambda k: jnp.stack([s[k] for s in new_s]).transpose(0, 2, 1, 3)
    return (hp.reshape(batch, seq, D_MODEL), y_sample,
            stack_p(0), stack_p(1), stack_p(2), stack_s(0), stack_s(1), stack_s(2))
```

```python
import functools

import jax
import jax.numpy as jnp
from jax import lax
from jax.experimental import pallas as pl
from jax.experimental.pallas import tpu as pltpu

D_MODEL = 1024
D_FF = 2816
PLE_DIM = 256
W_BR = 512
POOL_WINDOWS = (2, 4, 8, 16)
POOL_BUF = 15
SCONV_K = 3
CCONV_K = 31
PAST_LEN = 16384
EPS = 1e-6

LANES = 128
SUBLANES = 8
BF16_ROWS = 16
N_SLAB = W_BR // LANES
CONV_ROWS = 64

C_A, C_V, C_BG, C_CG, C_GA, C_GG, C_G0 = 0, 512, 1024, 1536, 2048, 2560, 3072
IN_COLS = C_G0 + 3 * D_MODEL
G_PART = 3 * D_MODEL // N_SLAB
C_F = C_GA
F_WIDTH = 2 * LANES + G_PART
IN_PERM = ((0, 0, C_GA),) + tuple(
    x for j in range(N_SLAB) for x in (
        (C_F + j * F_WIDTH, C_GA + j * LANES, LANES),
        (C_F + j * F_WIDTH + LANES, C_GG + j * LANES, LANES),
        (C_F + j * F_WIDTH + 2 * LANES, C_G0 + j * G_PART, G_PART)))

FF_CHUNKS = ((0, 1536), (1536, 1280))
MIX_CHUNK = 512

VMEM_LIMIT = 56 * 1024 * 1024

f32 = jnp.float32
bf16 = jnp.bfloat16


def _round_up(x, m):
    return (x + m - 1) // m * m


def _rms(x, g):
    return x * lax.rsqrt(jnp.mean(x * x, axis=-1, keepdims=True) + EPS) * g


def _dot(a, b):
    return jnp.dot(a, b, preferred_element_type=f32)


def _resident(shape):
    nd = len(shape)
    return pl.BlockSpec(shape, lambda *_: (0,) * nd, pipeline_mode=pl.Buffered(1))


def _layer(shape, i):
    nd = len(shape)
    return pl.BlockSpec((None,) + tuple(shape[1:]), lambda *_: (i,) + (0,) * (nd - 1), pipeline_mode=pl.Buffered(1))


def _spec(a, i):
    return _resident(a.shape) if i is None else _layer(a.shape, i)


def _params(ndim=1):
    return pltpu.CompilerParams(dimension_semantics=("arbitrary",) * ndim, vmem_limit_bytes=VMEM_LIMIT)


def _cast_plan(casts, nsteps, lin):
    ins, gains, outs, shapes, arrays, gain_arrays, scaled, perms = [], [], [], [], [], [], [], {}
    for k, (src, i, gain, *perm) in enumerate(casts):
        if perm:
            perms[k] = perm[0]
        _, r, c = src.shape
        div = 1
        while (r * div) % nsteps or (r * div // nsteps) % BF16_ROWS:
            div *= 2
        br = r * div // nsteps
        ins.append(pl.BlockSpec((None, br, c), lambda *g, i=i, div=div: (i, lin(*g) // div, 0)))
        outs.append(pl.BlockSpec((br, c), lambda *g, div=div: (lin(*g) // div, 0)))
        shapes.append(jax.ShapeDtypeStruct((r, c), bf16))
        arrays.append(src)
        if gain is not None:
            gains.append(pl.BlockSpec((None, br, 1), lambda *g, i=i, div=div: (i, lin(*g) // div, 0)))
            gain_arrays.append(gain.reshape(gain.shape[0], r, 1))
            scaled.append(k)
    return ins + gains, outs, shapes, arrays + gain_arrays, (tuple(scaled), tuple(perms.items()))


def _run_casts(srcs, dsts, meta):
    scaled, perms = meta
    gains = dict(zip(scaled, srcs[len(dsts):]))
    perms = dict(perms)
    for k, d in enumerate(dsts):
        for dst, src, width in perms.get(k, ((0, 0, d.shape[-1]),)):
            w = srcs[k][:, src:src + width]
            if k in gains:
                w = w * gains[k][...]
            d[:, dst:dst + width] = w.astype(bf16)


def _row_scale(x):
    return lax.rsqrt(jnp.mean(x * x, axis=-1, keepdims=True) + EPS)


def _swiglu_into(acc_scr, u_scr, rs, wg_ref, wu_ref, wd_ref):
    for n, (c0, cw) in enumerate(FF_CHUNKS):
        u = u_scr[...]
        g = _dot(u, wg_ref[:, c0:c0 + cw]) * rs
        up = _dot(u, wu_ref[:, c0:c0 + cw]) * rs
        a = (jax.nn.silu(g) * up).astype(bf16)
        d = _dot(a, wd_ref[c0:c0 + cw, :])
        if n == 0:
            acc_scr[...] = d
        else:
            acc_scr[...] += d


def _split_refs(refs, n_in, n_out, meta):
    n_cast_in = n_out + len(meta[0])
    return (refs[:n_in], refs[n_in:n_in + n_cast_in], refs[n_in + n_cast_in],
            refs[n_in + n_cast_in + 1:n_in + n_cast_in + 1 + n_out], refs[n_in + n_cast_in + 1 + n_out:])


def _ffn_kernel(*refs, n_cast, meta):
    (h_ref, wg_ref, wu_ref, wd_ref, gpost_ref), srcs, o_ref, dsts, (u_scr, acc_scr) = _split_refs(refs, 5, n_cast, meta)
    h = h_ref[...]
    u_scr[...] = h.astype(bf16)
    _swiglu_into(acc_scr, u_scr, _row_scale(h), wg_ref, wu_ref, wd_ref)
    o_ref[...] = h_ref[...] + _rms(acc_scr[...], 0.5 * gpost_ref[...])
    _run_casts(srcs, dsts, meta)


def _ffn_ple_kernel(*refs, n_cast, meta, final):
    ((h_ref, p_ref, wg_ref, wu_ref, wd_ref, gpost_ref, wpg_ref, wpp_ref, gppost_ref, gfin_ref),
     srcs, o_ref, dsts, (u_scr, acc_scr)) = _split_refs(refs, 10, n_cast, meta)
    h = h_ref[...]
    u_scr[...] = h.astype(bf16)
    _swiglu_into(acc_scr, u_scr, _row_scale(h), wg_ref, wu_ref, wd_ref)
    h = h_ref[...] + _rms(acc_scr[...], 0.5 * gpost_ref[...])
    gate = jax.nn.sigmoid(_dot(h.astype(bf16), wpg_ref[...]) * _row_scale(h))
    proj = _dot(p_ref[...].astype(bf16), wpp_ref[...])
    h = h + _rms(proj * gate, gppost_ref[...])
    if final:
        h = _rms(h, gfin_ref[...])
    o_ref[...] = h
    _run_casts(srcs, dsts, meta)


def _token_call(kern, name, h, extra, extra_specs, params, casts, *, tm):
    steps = h.shape[0] // tm
    row = pl.BlockSpec((tm, D_MODEL), lambda r: (r, 0))
    c_in, c_out, c_shapes, c_arrays, meta = _cast_plan(casts, steps, lambda r: r)
    return pl.pallas_call(
        functools.partial(kern, n_cast=len(casts), meta=meta),
        grid=(steps,),
        in_specs=[row] + extra_specs + [_spec(a, i) for a, i in params] + c_in,
        out_specs=[row] + c_out,
        out_shape=[jax.ShapeDtypeStruct(h.shape, f32)] + c_shapes,
        scratch_shapes=[pltpu.VMEM((tm, D_MODEL), bf16), pltpu.VMEM((tm, D_MODEL), f32)],
        compiler_params=_params(),
        name=name,
    )(h, *extra, *[a for a, _ in params], *c_arrays)


def _ffn_call(h, params, casts=(), *, tm):
    return _token_call(_ffn_kernel, "ffn", h, [], [], params, casts, tm=tm)


def _ffn_ple_call(h, p, i, params, casts=(), *, tm, final):
    prow = pl.BlockSpec((None, tm, PLE_DIM), lambda r: (i, r, 0))
    return _token_call(functools.partial(_ffn_ple_kernel, final=final), "ffn_ple", h, [p], [prow], params, casts, tm=tm)


OFF_POOL = _round_up(POOL_BUF, SUBLANES)
OFF_SC = _round_up(SCONV_K - 1, SUBLANES)
OFF_CF = _round_up(CCONV_K - 1, SUBLANES)


class _OneSequence:
    def __init__(self, l):
        self.l, self.tm = l, l
        self.blocks = tuple(range(0, l, CONV_ROWS))
        self.row_parts = ((0, l),)

    def blocks_in(self, r0, nr):
        return tuple(range(r0, r0 + nr, CONV_ROWS))

    def part_to_slab(self, xp, off, j, r0, nr, val):
        xp[j, off + r0:off + r0 + nr, :] = val

    def scratch(self, cols, dtype):
        return pltpu.VMEM((self.l, cols), dtype)

    def slab(self, off):
        return pltpu.VMEM((N_SLAB, off + self.l, LANES), f32)

    def load(self, ref, cs=slice(None)):
        return ref[:, cs]

    def store(self, ref, cs, val):
        ref[:, cs] = val

    def blk_load(self, ref, b, cs):
        return ref[b:b + CONV_ROWS, cs]

    def blk_store(self, ref, b, cs, val):
        ref[b:b + CONV_ROWS, cs] = val

    def win(self, xp, j, start, b):
        return xp[j, start + b:start + b + CONV_ROWS, :]

    def conv(self, xp, j, start, b, taps):
        acc = None
        for k, w in enumerate(taps):
            term = self.win(xp, j, start + k, b) * w
            acc = term if acc is None else acc + term
        return acc

    def tap(self, w_ref, k, cs):
        return w_ref[k:k + 1, cs]

    def to_slab(self, xp, off, j, val):
        xp[j, off:off + self.l, :] = val

    def row_index(self, b):
        return b + lax.broadcasted_iota(jnp.int32, (CONV_ROWS, LANES), 0)


class _StepMajor:
    def __init__(self, l, ns):
        self.l, self.ns, self.tm = l, ns, l * ns
        self.blocks = tuple(range(0, ns, BF16_ROWS))
        self.row_parts = ((0, self.tm),)

    def blocks_in(self, r0, nr):
        return self.blocks

    def part_to_slab(self, xp, off, j, r0, nr, val):
        self.to_slab(xp, off, j, val)

    def scratch(self, cols, dtype):
        return pltpu.VMEM((self.l, self.ns, cols), dtype)

    def slab(self, off):
        return pltpu.VMEM((N_SLAB, off + self.l, self.ns, LANES), f32)

    def load(self, ref, cs=slice(None)):
        v = ref[:, :, cs]
        return v.reshape(self.tm, v.shape[-1])

    def store(self, ref, cs, val):
        ref[:, :, cs] = val.reshape(self.l, self.ns, val.shape[-1])

    def blk_load(self, ref, b, cs):
        return ref[:, b:b + BF16_ROWS, cs]

    def blk_store(self, ref, b, cs, val):
        ref[:, b:b + BF16_ROWS, cs] = val

    def win(self, xp, j, start, b):
        return xp[j, start:start + self.l, b:b + BF16_ROWS, :]

    def conv(self, xp, j, start, b, taps):
        acc = None
        for k, w in enumerate(taps):
            term = self.win(xp, j, start + k, b) * w
            acc = term if acc is None else acc + term
        return acc

    def tap(self, w_ref, k, cs):
        return w_ref[k:k + 1, cs].reshape(1, 1, LANES)

    def to_slab(self, xp, off, j, val):
        xp[j, off:off + self.l, :, :] = val.reshape(self.l, self.ns, LANES)


def _mixer_body(lay, read_h, write_o, pos0,
                gpre_ref, win_ref, poolw_ref, pscale_ref, wpo_ref, scw_ref, wso_ref, cfw_ref, cfb_ref,
                lng_ref, lnb_ref, wco_ref, wo_ref, gpost_ref,
                u_scr, zb_scr, g_scr, y_scr, mm_scr, cc_scr, xp_pool, xp_sc, xp_cf):
    def _project_pool_shortconv():
        u_scr[...] = _rms(read_h(), gpre_ref[...]).astype(bf16)
        lay.store(zb_scr, slice(None), _dot(u_scr[...], win_ref[:, C_V:C_V + 3 * W_BR]))
        for j in range(N_SLAB):
            v = lay.load(zb_scr, slice(j * LANES, (j + 1) * LANES))
            cg = lay.load(zb_scr, slice(2 * W_BR + j * LANES, 2 * W_BR + (j + 1) * LANES))
            lay.to_slab(xp_sc, OFF_SC, j, cg * v)
        za = _dot(u_scr[...], win_ref[:, C_A:C_A + W_BR])
        for j in range(N_SLAB):
            lay.to_slab(xp_pool, OFF_POOL, j, za[:, j * LANES:(j + 1) * LANES])

        for j in range(N_SLAB):
            cs = slice(j * LANES, (j + 1) * LANES)
            taps = [lay.tap(scw_ref, k, cs) for k in range(SCONV_K)]
            for b in lay.blocks:
                cv = lay.conv(xp_sc, j, OFF_SC - (SCONV_K - 1), b, taps)
                bg = lay.blk_load(zb_scr, b, slice(W_BR + j * LANES, W_BR + (j + 1) * LANES))
                lay.blk_store(y_scr, b, slice(W_BR + j * LANES, W_BR + (j + 1) * LANES), (bg * cv).astype(bf16))

        for j, w in enumerate(POOL_WINDOWS):
            cs = slice(j * LANES, (j + 1) * LANES)
            for b in lay.blocks:
                cur = lay.win(xp_pool, j, OFF_POOL, b)
                s = cur
                for i in range(1, w):
                    s = s + lay.win(xp_pool, j, OFF_POOL - i, b)
                if pos0 is not None and b < w - 1:
                    mean = s / jnp.minimum(w, pos0 + lay.row_index(b) + 1).astype(f32)
                else:
                    mean = s * (1.0 / w)
                lay.blk_store(mm_scr, b, cs, (mean - cur).astype(bf16))
            ya = _dot(lay.load(mm_scr, cs), poolw_ref[j]) * pscale_ref[:, cs]
            lay.store(y_scr, cs, ya.astype(bf16))

    def _gates_and_conformer_conv():
        for j in range(N_SLAB):
            cs = slice(j * LANES, (j + 1) * LANES)
            taps = [lay.tap(cfw_ref, k, cs) for k in range(CCONV_K)]
            bias = cfb_ref[:, cs]
            for r0, nr in lay.row_parts:
                z = _dot(u_scr[r0:r0 + nr, :], win_ref[:, C_F + j * F_WIDTH:C_F + (j + 1) * F_WIDTH])
                glu = z[:, 0:LANES] * jax.nn.sigmoid(z[:, LANES:2 * LANES])
                lay.part_to_slab(xp_cf, OFF_CF, j, r0, nr, glu)
                g_scr[r0:r0 + nr, j * G_PART:(j + 1) * G_PART] = jax.nn.sigmoid(z[:, 2 * LANES:])
                for b in lay.blocks_in(r0, nr):
                    cc = lay.conv(xp_cf, j, OFF_CF - (CCONV_K - 1), b, taps)
                    lay.blk_store(cc_scr, b, cs, cc + bias)
        x = lay.load(cc_scr)
        xc = x - jnp.mean(x, axis=-1, keepdims=True)
        ln = xc * lax.rsqrt(jnp.mean(xc * xc, axis=-1, keepdims=True) + EPS) * lng_ref[...] + lnb_ref[...]
        lay.store(y_scr, slice(2 * W_BR, 3 * W_BR), jax.nn.silu(ln).astype(bf16))

    def _combine_and_project():
        m = []
        for c0 in range(0, D_MODEL, MIX_CHUNK):
            cs = slice(c0, c0 + MIX_CHUNK)
            acc = None
            for b, w_ref in enumerate((wpo_ref, wso_ref, wco_ref)):
                yb = lay.load(y_scr, slice(b * W_BR, (b + 1) * W_BR))
                term = g_scr[:, b * D_MODEL + c0:b * D_MODEL + c0 + MIX_CHUNK] * _dot(yb, w_ref[:, cs])
                acc = term if acc is None else acc + term
            m.append(acc.astype(bf16))
        mo = _dot(jnp.concatenate(m, axis=-1), wo_ref[...])
        write_o(read_h() + _rms(mo, gpost_ref[...]))

    _project_pool_shortconv()
    _gates_and_conformer_conv()
    _combine_and_project()


N_MIX_PARAMS = 14


def _mixer_scratch(lay):
    tm = lay.tm
    return [
        pltpu.VMEM((tm, D_MODEL), bf16),
        lay.scratch(3 * W_BR, f32),
        pltpu.VMEM((tm, 3 * D_MODEL), f32),
        lay.scratch(3 * W_BR, bf16),
        lay.scratch(W_BR, bf16),
        lay.scratch(W_BR, f32),
        lay.slab(OFF_POOL), lay.slab(OFF_SC), lay.slab(OFF_CF),
    ]


def _mixer_prompt_kernel(h_ref, *refs, lay, n_cast, meta):
    params = refs[:N_MIX_PARAMS]
    n_src = n_cast + len(meta[0])
    srcs = refs[N_MIX_PARAMS:N_MIX_PARAMS + n_src]
    o_ref, npool_ref, nsc_ref, ncf_ref = refs[N_MIX_PARAMS + n_src:N_MIX_PARAMS + n_src + 4]
    dsts = refs[N_MIX_PARAMS + n_src + 4:N_MIX_PARAMS + n_src + n_cast + 4]
    scr = refs[N_MIX_PARAMS + n_src + n_cast + 4:]
    xp_pool, xp_sc, xp_cf = scr[-3:]
    l = lay.l
    c = pl.program_id(1)
    carried = ((xp_pool, OFF_POOL, POOL_BUF, npool_ref), (xp_sc, OFF_SC, SCONV_K - 1, nsc_ref),
               (xp_cf, OFF_CF, CCONV_K - 1, ncf_ref))

    @pl.when(c == 0)
    def _():
        for xp, off, _, _ in carried:
            xp[:, 0:off, :] = jnp.zeros((N_SLAB, off, LANES), f32)

    def write_o(val):
        o_ref[...] = val

    _mixer_body(lay, lambda: h_ref[...], write_o, c * l, *params, *scr)
    _run_casts(srcs, dsts, meta)

    for xp, off, nb, out_ref in carried:
        xp[:, off - nb:off, :] = xp[:, off + l - nb:off + l, :]
        for j in range(N_SLAB):
            out_ref[0, :, j * LANES:(j + 1) * LANES] = xp[j, off - nb:off, :]


def _mixer_sample_kernel(h_ref, spool_ref, ssc_ref, scf_ref, *refs, lay):
    params = refs[:N_MIX_PARAMS]
    o_ref, npool_ref, nsc_ref, ncf_ref = refs[N_MIX_PARAMS:N_MIX_PARAMS + 4]
    scr = refs[N_MIX_PARAMS + 4:]
    xp_pool, xp_sc, xp_cf = scr[-3:]
    l = lay.l
    carried = ((xp_pool, OFF_POOL, POOL_BUF, spool_ref, npool_ref), (xp_sc, OFF_SC, SCONV_K - 1, ssc_ref, nsc_ref),
               (xp_cf, OFF_CF, CCONV_K - 1, scf_ref, ncf_ref))
    for xp, off, nb, in_ref, _ in carried:
        for j in range(N_SLAB):
            xp[j, off - nb:off, :, :] = in_ref[:, :, j * LANES:(j + 1) * LANES]
    assert PAST_LEN + 1 >= max(POOL_WINDOWS)

    def write_o(val):
        o_ref[...] = val.reshape(l, lay.ns, D_MODEL)

    _mixer_body(lay, lambda: h_ref[...].reshape(lay.tm, D_MODEL), write_o, None, *params, *scr)
    for xp, off, nb, _, out_ref in carried:
        for j in range(N_SLAB):
            out_ref[:, :, j * LANES:(j + 1) * LANES] = xp[j, off + l - nb:off + l, :, :]


def _mixer_prompt_call(h, params, casts, *, batch, seq, l):
    ch = seq // l
    lay = _OneSequence(l)
    row = pl.BlockSpec((l, D_MODEL), lambda b, c: (b * ch + c, 0))
    c_in, c_out, c_shapes, c_arrays, meta = _cast_plan(casts, batch * ch, lambda b, c: b * ch + c)

    def state_spec(nb):
        return pl.BlockSpec((1, nb, W_BR), lambda b, c: (b, 0, 0))

    return pl.pallas_call(
        functools.partial(_mixer_prompt_kernel, lay=lay, n_cast=len(casts), meta=meta),
        grid=(batch, ch),
        in_specs=[row] + [_spec(a, i) for a, i in params] + c_in,
        out_specs=[row, state_spec(POOL_BUF), state_spec(SCONV_K - 1), state_spec(CCONV_K - 1)] + c_out,
        out_shape=[jax.ShapeDtypeStruct(h.shape, f32),
                   jax.ShapeDtypeStruct((batch, POOL_BUF, W_BR), f32),
                   jax.ShapeDtypeStruct((batch, SCONV_K - 1, W_BR), f32),
                   jax.ShapeDtypeStruct((batch, CCONV_K - 1, W_BR), f32)] + c_shapes,
        scratch_shapes=_mixer_scratch(lay),
        compiler_params=_params(ndim=2),
        name="mixer_prompt",
    )(h, *[a for a, _ in params], *c_arrays)


def _mixer_sample_call(h, i, spool, ssc, scf, params, *, ns):
    l, nseq, _ = h.shape
    lay = _StepMajor(l, ns)
    row = pl.BlockSpec((l, ns, D_MODEL), lambda s: (0, s, 0))

    def state_in(nb):
        return pl.BlockSpec((None, nb, ns, W_BR), lambda s: (i, 0, s, 0))

    def state_out(nb):
        return pl.BlockSpec((nb, ns, W_BR), lambda s: (0, s, 0))

    nbs = (POOL_BUF, SCONV_K - 1, CCONV_K - 1)
    return pl.pallas_call(
        functools.partial(_mixer_sample_kernel, lay=lay),
        grid=(nseq // ns,),
        in_specs=[row] + [state_in(nb) for nb in nbs] + [_spec(a, j) for a, j in params],
        out_specs=[row] + [state_out(nb) for nb in nbs],
        out_shape=[jax.ShapeDtypeStruct(h.shape, f32)] + [jax.ShapeDtypeStruct((nb, nseq, W_BR), f32) for nb in nbs],
        scratch_shapes=_mixer_scratch(lay),
        compiler_params=_params(),
        name="mixer_sample",
    )(h, spool, ssc, scf, *[a for a, _ in params])


def kernel(x_prompt, x_sample, p_prompt, p_sample, state_pool, state_sconv, state_cconv, g_f1_pre, w_f1_gate, w_f1_up, w_f1_down, g_f1_post, g_mix_pre, w_in, pool_w, pool_scale, w_pool_out, sc_conv, w_sc_out, cf_conv, cf_conv_b, cf_ln_g, cf_ln_b, w_cf_out, w_o, g_mix_post, g_f2_pre, w_f2_gate, w_f2_up, w_f2_down, g_f2_post, g_ple_pre, w_ple_gate, w_ple_proj, g_ple_post, g_final):
    depth = w_in.shape[0]
    batch, seq, _ = x_prompt.shape
    nseq, dec_seq, _ = x_sample.shape
    tm = 512
    tm_tok = 1024
    ns = 32

    def vec(a, i):
        return (a.reshape(depth, 1, a.shape[-1]), i)

    def whole(a):
        return (a, None)

    gfin = whole(g_final.reshape(1, -1))
    pool_w2 = pool_w.reshape(depth, W_BR, LANES)

    hp = x_prompt.reshape(batch * seq, D_MODEL)
    hs = x_sample.transpose(1, 0, 2).reshape(dec_seq * nseq, D_MODEL)
    pp = p_prompt.reshape(depth, batch * seq, PLE_DIM)
    ps = p_sample.transpose(0, 2, 1, 3).reshape(depth, dec_seq * nseq, PLE_DIM)
    spool, ssc, scf = (s.transpose(0, 2, 1, 3) for s in (state_pool, state_sconv, state_cconv))

    f1_w = [(w_f1_gate[0] * g_f1_pre[0][:, None]).astype(bf16), (w_f1_up[0] * g_f1_pre[0][:, None]).astype(bf16),
            w_f1_down[0].astype(bf16)]
    new_p, new_s = [], []
    for i in range(depth):
        f1 = [whole(w) for w in f1_w] + [vec(g_f1_post, i)]
        mix_src = (w_in, pool_w2, w_pool_out, w_sc_out, w_cf_out, w_o)
        hp, *mix_w = _ffn_call(hp, f1, [(w_in, i, None, IN_PERM)] + [(w, i, None) for w in mix_src[1:]], tm=tm_tok)
        hs, = _ffn_call(hs, f1, tm=tm)
        wi, pw, wpo, wso, wco, wo = mix_w
        mix = [vec(g_mix_pre, i), whole(wi), whole(pw.reshape(len(POOL_WINDOWS), LANES, LANES)), vec(pool_scale, i),
               whole(wpo), (sc_conv, i), whole(wso), (cf_conv, i), vec(cf_conv_b, i), vec(cf_ln_g, i),
               vec(cf_ln_b, i), whole(wco), whole(wo), vec(g_mix_post, i)]
        f2_src = ((w_f2_gate, i, g_f2_pre), (w_f2_up, i, g_f2_pre), (w_f2_down, i, None),
                  (w_ple_gate, i, g_ple_pre), (w_ple_proj, i, None))
        hp, *rest = _mixer_prompt_call(hp, mix, f2_src, batch=batch, seq=seq, l=tm)
        st_p, f2_w = rest[:3], rest[3:]
        hs, *st_s = _mixer_sample_call(hs.reshape(dec_seq, nseq, D_MODEL), i, spool, ssc, scf, mix, ns=ns)
        hs = hs.reshape(dec_seq * nseq, D_MODEL)
        new_p.append(st_p)
        new_s.append(st_s)
        wg, wu, wd, wpg, wpp = f2_w
        f2 = [whole(wg), whole(wu), whole(wd), vec(g_f2_post, i), whole(wpg), whole(wpp), vec(g_ple_post, i), gfin]
        final = i == depth - 1
        nxt = [] if final else [(w_f1_gate, i + 1, g_f1_pre), (w_f1_up, i + 1, g_f1_pre), (w_f1_down, i + 1, None)]
        hp, *f1_w = _ffn_ple_call(hp, pp, i, f2, nxt, tm=tm_tok, final=final)
        hs, = _ffn_ple_call(hs, ps, i, f2, tm=tm, final=final)
    y_sample = hs.reshape(dec_seq, nseq, D_MODEL).transpose(1, 0, 2)
    stack_p = lambda k: jnp.stack([s[k] for s in new_p])
    stack_s = lambda k: jnp.stack([s[k] for s in new_s]).transpose(0, 2, 1, 3)
    return (hp.reshape(batch, seq, D_MODEL), y_sample,
            stack_p(0), stack_p(1), stack_p(2), stack_s(0), stack_s(1), stack_s(2))
```

```python
import functools

import jax
import jax.numpy as jnp
from jax import lax
from jax.experimental import pallas as pl
from jax.experimental.pallas import tpu as pltpu

D_MODEL = 1024
D_FF = 2816
PLE_DIM = 256
W_BR = 512
POOL_WINDOWS = (2, 4, 8, 16)
POOL_BUF = 15
SCONV_K = 3
CCONV_K = 31
PAST_LEN = 16384
EPS = 1e-6

LANES = 128
SUBLANES = 8
BF16_ROWS = 16
N_SLAB = W_BR // LANES
CONV_ROWS = 64

C_A, C_V, C_BG, C_CG, C_GA, C_GG, C_G0 = 0, 512, 1024, 1536, 2048, 2560, 3072
IN_COLS = C_G0 + 3 * D_MODEL
G_PART = 3 * D_MODEL // N_SLAB
F_SRC = (C_A, C_V, C_BG, C_CG, C_GA, C_GG)
F_WIDTH = len(F_SRC) * LANES + G_PART
IN_PERM = tuple(
    x for j in range(N_SLAB) for x in (
        tuple((j * F_WIDTH + k * LANES, c + j * LANES, LANES) for k, c in enumerate(F_SRC))
        + ((j * F_WIDTH + len(F_SRC) * LANES, C_G0 + j * G_PART, G_PART),)))

FF_CHUNKS = ((0, 1024), (1024, 1024), (2048, 768))
MIX_CHUNK = 512

VMEM_LIMIT = 56 * 1024 * 1024

f32 = jnp.float32
bf16 = jnp.bfloat16


def _round_up(x, m):
    return (x + m - 1) // m * m


def _rms(x, g):
    return x * lax.rsqrt(jnp.mean(x * x, axis=-1, keepdims=True) + EPS) * g


def _dot(a, b):
    return jnp.dot(a, b, preferred_element_type=f32)


def _resident(shape):
    nd = len(shape)
    return pl.BlockSpec(shape, lambda *_: (0,) * nd, pipeline_mode=pl.Buffered(1))


def _layer(shape, i):
    nd = len(shape)
    return pl.BlockSpec((None,) + tuple(shape[1:]), lambda *_: (i,) + (0,) * (nd - 1), pipeline_mode=pl.Buffered(1))


def _spec(a, i):
    return _resident(a.shape) if i is None else _layer(a.shape, i)


def _params(ndim=1):
    return pltpu.CompilerParams(dimension_semantics=("arbitrary",) * ndim, vmem_limit_bytes=VMEM_LIMIT)


def _cast_plan(casts, nsteps, lin):
    ins, gains, outs, shapes, arrays, gain_arrays, scaled, perms = [], [], [], [], [], [], [], {}
    for k, (src, i, gain, *perm) in enumerate(casts):
        if perm:
            perms[k] = perm[0]
        _, r, c = src.shape
        div = 1
        while (r * div) % nsteps or (r * div // nsteps) % BF16_ROWS:
            div *= 2
        br = r * div // nsteps
        ins.append(pl.BlockSpec((None, br, c), lambda *g, i=i, div=div: (i, lin(*g) // div, 0)))
        outs.append(pl.BlockSpec((br, c), lambda *g, div=div: (lin(*g) // div, 0)))
        shapes.append(jax.ShapeDtypeStruct((r, c), bf16))
        arrays.append(src)
        if gain is not None:
            gains.append(pl.BlockSpec((None, br, 1), lambda *g, i=i, div=div: (i, lin(*g) // div, 0)))
            gain_arrays.append(gain.reshape(gain.shape[0], r, 1))
            scaled.append(k)
    return ins + gains, outs, shapes, arrays + gain_arrays, (tuple(scaled), tuple(perms.items()))


def _run_casts(srcs, dsts, meta):
    scaled, perms = meta
    gains = dict(zip(scaled, srcs[len(dsts):]))
    perms = dict(perms)
    for k, d in enumerate(dsts):
        for dst, src, width in perms.get(k, ((0, 0, d.shape[-1]),)):
            w = srcs[k][:, src:src + width]
            if k in gains:
                w = w * gains[k][...]
            d[:, dst:dst + width] = w.astype(bf16)


def _row_scale(x):
    return lax.rsqrt(jnp.mean(x * x, axis=-1, keepdims=True) + EPS)


def _swiglu_into(acc_scr, u_scr, rs, wg_ref, wu_ref, wd_ref):
    for n, (c0, cw) in enumerate(FF_CHUNKS):
        u = u_scr[...]
        g = _dot(u, wg_ref[:, c0:c0 + cw]) * rs
        up = _dot(u, wu_ref[:, c0:c0 + cw]) * rs
        a = (jax.nn.silu(g) * up).astype(bf16)
        d = _dot(a, wd_ref[c0:c0 + cw, :])
        if n == 0:
            acc_scr[...] = d
        else:
            acc_scr[...] += d


def _split_refs(refs, n_in, n_out, meta):
    n_cast_in = n_out + len(meta[0])
    return (refs[:n_in], refs[n_in:n_in + n_cast_in], refs[n_in + n_cast_in],
            refs[n_in + n_cast_in + 1:n_in + n_cast_in + 1 + n_out], refs[n_in + n_cast_in + 1 + n_out:])


def _ffn_kernel(*refs, n_cast, meta):
    (h_ref, wg_ref, wu_ref, wd_ref, gpost_ref), srcs, o_ref, dsts, (u_scr, acc_scr) = _split_refs(refs, 5, n_cast, meta)
    h = h_ref[...]
    u_scr[...] = h.astype(bf16)
    _swiglu_into(acc_scr, u_scr, _row_scale(h), wg_ref, wu_ref, wd_ref)
    o_ref[...] = h_ref[...] + _rms(acc_scr[...], 0.5 * gpost_ref[...])
    _run_casts(srcs, dsts, meta)


def _ffn_ple_kernel(*refs, n_cast, meta, final):
    ((h_ref, p_ref, wg_ref, wu_ref, wd_ref, gpost_ref, wpg_ref, wpp_ref, gppost_ref, gfin_ref),
     srcs, o_ref, dsts, (u_scr, acc_scr)) = _split_refs(refs, 10, n_cast, meta)
    h = h_ref[...]
    u_scr[...] = h.astype(bf16)
    _swiglu_into(acc_scr, u_scr, _row_scale(h), wg_ref, wu_ref, wd_ref)
    h = h_ref[...] + _rms(acc_scr[...], 0.5 * gpost_ref[...])
    gate = jax.nn.sigmoid(_dot(h.astype(bf16), wpg_ref[...]) * _row_scale(h))
    proj = _dot(p_ref[...].astype(bf16), wpp_ref[...])
    h = h + _rms(proj * gate, gppost_ref[...])
    if final:
        h = _rms(h, gfin_ref[...])
    o_ref[...] = h
    _run_casts(srcs, dsts, meta)


def _token_call(kern, name, h, extra, extra_specs, params, casts, *, tm):
    steps = h.shape[0] // tm
    row = pl.BlockSpec((tm, D_MODEL), lambda r: (r, 0))
    c_in, c_out, c_shapes, c_arrays, meta = _cast_plan(casts, steps, lambda r: r)
    return pl.pallas_call(
        functools.partial(kern, n_cast=len(casts), meta=meta),
        grid=(steps,),
        in_specs=[row] + extra_specs + [_spec(a, i) for a, i in params] + c_in,
        out_specs=[row] + c_out,
        out_shape=[jax.ShapeDtypeStruct(h.shape, f32)] + c_shapes,
        scratch_shapes=[pltpu.VMEM((tm, D_MODEL), bf16), pltpu.VMEM((tm, D_MODEL), f32)],
        compiler_params=_params(),
        name=name,
    )(h, *extra, *[a for a, _ in params], *c_arrays)


def _ffn_call(h, params, casts=(), *, tm):
    return _token_call(_ffn_kernel, "ffn", h, [], [], params, casts, tm=tm)


def _ffn_ple_call(h, p, i, params, casts=(), *, tm, final):
    prow = pl.BlockSpec((None, tm, PLE_DIM), lambda r: (i, r, 0))
    return _token_call(functools.partial(_ffn_ple_kernel, final=final), "ffn_ple", h, [p], [prow], params, casts, tm=tm)


OFF_POOL = _round_up(POOL_BUF, SUBLANES)
OFF_SC = _round_up(SCONV_K - 1, SUBLANES)
OFF_CF = _round_up(CCONV_K - 1, SUBLANES)


class _OneSequence:
    def __init__(self, l):
        self.l, self.tm = l, l
        self.blocks = tuple(range(0, l, CONV_ROWS))
        self.row_parts = tuple((r0, l // 2) for r0 in (0, l // 2))

    def blocks_in(self, r0, nr):
        return tuple(range(r0, r0 + nr, CONV_ROWS))

    def part_to_slab(self, xp, off, j, r0, nr, val):
        xp[j, off + r0:off + r0 + nr, :] = val

    def part_store(self, ref, r0, nr, cs, val):
        ref[r0:r0 + nr, cs] = val

    def scratch(self, cols, dtype):
        return pltpu.VMEM((self.l, cols), dtype)

    def slab(self, off):
        return pltpu.VMEM((N_SLAB, off + self.l, LANES), f32)

    def load(self, ref, cs=slice(None)):
        return ref[:, cs]

    def store(self, ref, cs, val):
        ref[:, cs] = val

    def blk_load(self, ref, b, cs):
        return ref[b:b + CONV_ROWS, cs]

    def blk_store(self, ref, b, cs, val):
        ref[b:b + CONV_ROWS, cs] = val

    def win(self, xp, j, start, b):
        return xp[j, start + b:start + b + CONV_ROWS, :]

    def conv(self, xp, j, start, b, taps):
        acc = None
        for k, w in enumerate(taps):
            term = self.win(xp, j, start + k, b) * w
            acc = term if acc is None else acc + term
        return acc

    def tap(self, w_ref, k, cs):
        return w_ref[k:k + 1, cs]

    def to_slab(self, xp, off, j, val):
        xp[j, off:off + self.l, :] = val

    def row_index(self, b):
        return b + lax.broadcasted_iota(jnp.int32, (CONV_ROWS, LANES), 0)


class _StepMajor:
    def __init__(self, l, ns):
        self.l, self.ns, self.tm = l, ns, l * ns
        self.blocks = tuple(range(0, ns, BF16_ROWS))
        self.row_parts = ((0, self.tm),)

    def blocks_in(self, r0, nr):
        return self.blocks

    def part_to_slab(self, xp, off, j, r0, nr, val):
        self.to_slab(xp, off, j, val)

    def part_store(self, ref, r0, nr, cs, val):
        self.store(ref, cs, val)

    def scratch(self, cols, dtype):
        return pltpu.VMEM((self.l, self.ns, cols), dtype)

    def slab(self, off):
        return pltpu.VMEM((N_SLAB, off + self.l, self.ns, LANES), f32)

    def load(self, ref, cs=slice(None)):
        v = ref[:, :, cs]
        return v.reshape(self.tm, v.shape[-1])

    def store(self, ref, cs, val):
        ref[:, :, cs] = val.reshape(self.l, self.ns, val.shape[-1])

    def blk_load(self, ref, b, cs):
        return ref[:, b:b + BF16_ROWS, cs]

    def blk_store(self, ref, b, cs, val):
        ref[:, b:b + BF16_ROWS, cs] = val

    def win(self, xp, j, start, b):
        return xp[j, start:start + self.l, b:b + BF16_ROWS, :]

    def conv(self, xp, j, start, b, taps):
        acc = None
        for k, w in enumerate(taps):
            term = self.win(xp, j, start + k, b) * w
            acc = term if acc is None else acc + term
        return acc

    def tap(self, w_ref, k, cs):
        return w_ref[k:k + 1, cs].reshape(1, 1, LANES)

    def to_slab(self, xp, off, j, val):
        xp[j, off:off + self.l, :, :] = val.reshape(self.l, self.ns, LANES)


def _mixer_body(lay, read_h, write_o, pos0,
                gpre_ref, win_ref, poolw_ref, pscale_ref, wpo_ref, scw_ref, wso_ref, cfw_ref, cfb_ref,
                lng_ref, lnb_ref, wco_ref, wo_ref, gpost_ref,
                u_scr, zb_scr, g_scr, y_scr, mm_scr, cc_scr, xp_pool, xp_sc, xp_cf):
    def _gates_and_conformer_conv():
        u_scr[...] = _rms(read_h(), gpre_ref[...]).astype(bf16)
        for j, w in enumerate(POOL_WINDOWS):
            cs = slice(j * LANES, (j + 1) * LANES)
            taps_s = [lay.tap(scw_ref, k, cs) for k in range(SCONV_K)]
            taps_c = [lay.tap(cfw_ref, k, cs) for k in range(CCONV_K)]
            bias = cfb_ref[:, cs]
            for r0, nr in lay.row_parts:
                z = _dot(u_scr[r0:r0 + nr, :], win_ref[:, j * F_WIDTH:(j + 1) * F_WIDTH])
                a, v, bg, cg, ga, gg = (z[:, k * LANES:(k + 1) * LANES] for k in range(len(F_SRC)))
                lay.part_to_slab(xp_pool, OFF_POOL, j, r0, nr, a)
                lay.part_to_slab(xp_sc, OFF_SC, j, r0, nr, cg * v)
                lay.part_to_slab(xp_cf, OFF_CF, j, r0, nr, ga * jax.nn.sigmoid(gg))
                lay.part_store(zb_scr, r0, nr, cs, bg)
                g_scr[r0:r0 + nr, j * G_PART:(j + 1) * G_PART] = jax.nn.sigmoid(z[:, len(F_SRC) * LANES:])
                for b in lay.blocks_in(r0, nr):
                    cv = lay.conv(xp_sc, j, OFF_SC - (SCONV_K - 1), b, taps_s)
                    ycs = slice(W_BR + j * LANES, W_BR + (j + 1) * LANES)
                    lay.blk_store(y_scr, b, ycs, (lay.blk_load(zb_scr, b, cs) * cv).astype(bf16))
                    cur = lay.win(xp_pool, j, OFF_POOL, b)
                    s = cur
                    for i in range(1, w):
                        s = s + lay.win(xp_pool, j, OFF_POOL - i, b)
                    if pos0 is not None and b < w - 1:
                        mean = s / jnp.minimum(w, pos0 + lay.row_index(b) + 1).astype(f32)
                    else:
                        mean = s * (1.0 / w)
                    lay.blk_store(mm_scr, b, cs, (mean - cur).astype(bf16))
                    cc = lay.conv(xp_cf, j, OFF_CF - (CCONV_K - 1), b, taps_c)
                    lay.blk_store(cc_scr, b, cs, cc + bias)
            ya = _dot(lay.load(mm_scr, cs), poolw_ref[j]) * pscale_ref[:, cs]
            lay.store(y_scr, cs, ya.astype(bf16))
        x = lay.load(cc_scr)
        xc = x - jnp.mean(x, axis=-1, keepdims=True)
        ln = xc * lax.rsqrt(jnp.mean(xc * xc, axis=-1, keepdims=True) + EPS) * lng_ref[...] + lnb_ref[...]
        lay.store(y_scr, slice(2 * W_BR, 3 * W_BR), jax.nn.silu(ln).astype(bf16))

    def _combine_and_project():
        m = []
        for c0 in range(0, D_MODEL, MIX_CHUNK):
            cs = slice(c0, c0 + MIX_CHUNK)
            acc = None
            for b, w_ref in enumerate((wpo_ref, wso_ref, wco_ref)):
                yb = lay.load(y_scr, slice(b * W_BR, (b + 1) * W_BR))
                term = g_scr[:, b * D_MODEL + c0:b * D_MODEL + c0 + MIX_CHUNK] * _dot(yb, w_ref[:, cs])
                acc = term if acc is None else acc + term
            m.append(acc.astype(bf16))
        mo = _dot(jnp.concatenate(m, axis=-1), wo_ref[...])
        write_o(read_h() + _rms(mo, gpost_ref[...]))

    _gates_and_conformer_conv()
    _combine_and_project()


N_MIX_PARAMS = 14


def _mixer_scratch(lay):
    tm = lay.tm
    return [
        pltpu.VMEM((tm, D_MODEL), bf16),
        lay.scratch(W_BR, f32),
        pltpu.VMEM((tm, 3 * D_MODEL), f32),
        lay.scratch(3 * W_BR, bf16),
        lay.scratch(W_BR, bf16),
        lay.scratch(W_BR, f32),
        lay.slab(OFF_POOL), lay.slab(OFF_SC), lay.slab(OFF_CF),
    ]


def _mixer_prompt_kernel(h_ref, *refs, lay, n_cast, meta):
    params = refs[:N_MIX_PARAMS]
    n_src = n_cast + len(meta[0])
    srcs = refs[N_MIX_PARAMS:N_MIX_PARAMS + n_src]
    o_ref, npool_ref, nsc_ref, ncf_ref = refs[N_MIX_PARAMS + n_src:N_MIX_PARAMS + n_src + 4]
    dsts = refs[N_MIX_PARAMS + n_src + 4:N_MIX_PARAMS + n_src + n_cast + 4]
    scr = refs[N_MIX_PARAMS + n_src + n_cast + 4:]
    xp_pool, xp_sc, xp_cf = scr[-3:]
    l = lay.l
    c = pl.program_id(1)
    carried = ((xp_pool, OFF_POOL, POOL_BUF, npool_ref), (xp_sc, OFF_SC, SCONV_K - 1, nsc_ref),
               (xp_cf, OFF_CF, CCONV_K - 1, ncf_ref))

    @pl.when(c == 0)
    def _():
        for xp, off, _, _ in carried:
            xp[:, 0:off, :] = jnp.zeros((N_SLAB, off, LANES), f32)

    def write_o(val):
        o_ref[...] = val

    _mixer_body(lay, lambda: h_ref[...], write_o, c * l, *params, *scr)
    _run_casts(srcs, dsts, meta)

    for xp, off, nb, out_ref in carried:
        xp[:, off - nb:off, :] = xp[:, off + l - nb:off + l, :]
        for j in range(N_SLAB):
            out_ref[0, :, j * LANES:(j + 1) * LANES] = xp[j, off - nb:off, :]


def _mixer_sample_kernel(h_ref, spool_ref, ssc_ref, scf_ref, *refs, lay):
    params = refs[:N_MIX_PARAMS]
    o_ref, npool_ref, nsc_ref, ncf_ref = refs[N_MIX_PARAMS:N_MIX_PARAMS + 4]
    scr = refs[N_MIX_PARAMS + 4:]
    xp_pool, xp_sc, xp_cf = scr[-3:]
    l = lay.l
    carried = ((xp_pool, OFF_POOL, POOL_BUF, spool_ref, npool_ref), (xp_sc, OFF_SC, SCONV_K - 1, ssc_ref, nsc_ref),
               (xp_cf, OFF_CF, CCONV_K - 1, scf_ref, ncf_ref))
    for xp, off, nb, in_ref, _ in carried:
        for j in range(N_SLAB):
            xp[j, off - nb:off, :, :] = in_ref[:, :, j * LANES:(j + 1) * LANES]
    assert PAST_LEN + 1 >= max(POOL_WINDOWS)

    def write_o(val):
        o_ref[...] = val.reshape(l, lay.ns, D_MODEL)

    _mixer_body(lay, lambda: h_ref[...].reshape(lay.tm, D_MODEL), write_o, None, *params, *scr)
    for xp, off, nb, _, out_ref in carried:
        for j in range(N_SLAB):
            out_ref[:, :, j * LANES:(j + 1) * LANES] = xp[j, off + l - nb:off + l, :, :]


def _mixer_prompt_call(h, params, casts, *, batch, seq, l):
    ch = seq // l
    lay = _OneSequence(l)
    row = pl.BlockSpec((l, D_MODEL), lambda b, c: (b * ch + c, 0))
    c_in, c_out, c_shapes, c_arrays, meta = _cast_plan(casts, batch * ch, lambda b, c: b * ch + c)

    def state_spec(nb):
        return pl.BlockSpec((1, nb, W_BR), lambda b, c: (b, 0, 0))

    return pl.pallas_call(
        functools.partial(_mixer_prompt_kernel, lay=lay, n_cast=len(casts), meta=meta),
        grid=(batch, ch),
        in_specs=[row] + [_spec(a, i) for a, i in params] + c_in,
        out_specs=[row, state_spec(POOL_BUF), state_spec(SCONV_K - 1), state_spec(CCONV_K - 1)] + c_out,
        out_shape=[jax.ShapeDtypeStruct(h.shape, f32),
                   jax.ShapeDtypeStruct((batch, POOL_BUF, W_BR), f32),
                   jax.ShapeDtypeStruct((batch, SCONV_K - 1, W_BR), f32),
                   jax.ShapeDtypeStruct((batch, CCONV_K - 1, W_BR), f32)] + c_shapes,
        scratch_shapes=_mixer_scratch(lay),
        compiler_params=_params(ndim=2),
        name="mixer_prompt",
    )(h, *[a for a, _ in params], *c_arrays)


def _mixer_sample_call(h, i, spool, ssc, scf, params, *, ns):
    l, nseq, _ = h.shape
    lay = _StepMajor(l, ns)
    row = pl.BlockSpec((l, ns, D_MODEL), lambda s: (0, s, 0))

    def state_in(nb):
        return pl.BlockSpec((None, nb, ns, W_BR), lambda s: (i, 0, s, 0))

    def state_out(nb):
        return pl.BlockSpec((nb, ns, W_BR), lambda s: (0, s, 0))

    nbs = (POOL_BUF, SCONV_K - 1, CCONV_K - 1)
    return pl.pallas_call(
        functools.partial(_mixer_sample_kernel, lay=lay),
        grid=(nseq // ns,),
        in_specs=[row] + [state_in(nb) for nb in nbs] + [_spec(a, j) for a, j in params],
        out_specs=[row] + [state_out(nb) for nb in nbs],
        out_shape=[jax.ShapeDtypeStruct(h.shape, f32)] + [jax.ShapeDtypeStruct((nb, nseq, W_BR), f32) for nb in nbs],
        scratch_shapes=_mixer_scratch(lay),
        compiler_params=_params(),
        name="mixer_sample",
    )(h, spool, ssc, scf, *[a for a, _ in params])


def kernel(x_prompt, x_sample, p_prompt, p_sample, state_pool, state_sconv, state_cconv, g_f1_pre, w_f1_gate, w_f1_up, w_f1_down, g_f1_post, g_mix_pre, w_in, pool_w, pool_scale, w_pool_out, sc_conv, w_sc_out, cf_conv, cf_conv_b, cf_ln_g, cf_ln_b, w_cf_out, w_o, g_mix_post, g_f2_pre, w_f2_gate, w_f2_up, w_f2_down, g_f2_post, g_ple_pre, w_ple_gate, w_ple_proj, g_ple_post, g_final):
    depth = w_in.shape[0]
    batch, seq, _ = x_prompt.shape
    nseq, dec_seq, _ = x_sample.shape
    tm = 512
    tm_tok = 1024
    ns = 32

    def vec(a, i):
        return (a.reshape(depth, 1, a.shape[-1]), i)

    def whole(a):
        return (a, None)

    gfin = whole(g_final.reshape(1, -1))
    pool_w2 = pool_w.reshape(depth, W_BR, LANES)

    hp = x_prompt.reshape(batch * seq, D_MODEL)
    hs = x_sample.transpose(1, 0, 2).reshape(dec_seq * nseq, D_MODEL)
    pp = p_prompt.reshape(depth, batch * seq, PLE_DIM)
    ps = p_sample.transpose(0, 2, 1, 3).reshape(depth, dec_seq * nseq, PLE_DIM)
    spool, ssc, scf = (s.transpose(0, 2, 1, 3) for s in (state_pool, state_sconv, state_cconv))

    f1_w = [(w_f1_gate[0] * g_f1_pre[0][:, None]).astype(bf16), (w_f1_up[0] * g_f1_pre[0][:, None]).astype(bf16),
            w_f1_down[0].astype(bf16)]
    new_p, new_s = [], []
    for i in range(depth):
        f1 = [whole(w) for w in f1_w] + [vec(g_f1_post, i)]
        mix_src = (w_in, pool_w2, w_pool_out, w_sc_out, w_cf_out, w_o)
        hp, *mix_w = _ffn_call(hp, f1, [(w_in, i, None, IN_PERM)] + [(w, i, None) for w in mix_src[1:]], tm=tm_tok)
        hs, = _ffn_call(hs, f1, tm=tm)
        wi, pw, wpo, wso, wco, wo = mix_w
        mix = [vec(g_mix_pre, i), whole(wi), whole(pw.reshape(len(POOL_WINDOWS), LANES, LANES)), vec(pool_scale, i),
               whole(wpo), (sc_conv, i), whole(wso), (cf_conv, i), vec(cf_conv_b, i), vec(cf_ln_g, i),
               vec(cf_ln_b, i), whole(wco), whole(wo), vec(g_mix_post, i)]
        f2_src = ((w_f2_gate, i, g_f2_pre), (w_f2_up, i, g_f2_pre), (w_f2_down, i, None),
                  (w_ple_gate, i, g_ple_pre), (w_ple_proj, i, None))
        hp, *rest = _mixer_prompt_call(hp, mix, f2_src, batch=batch, seq=seq, l=tm)
        st_p, f2_w = rest[:3], rest[3:]
        hs, *st_s = _mixer_sample_call(hs.reshape(dec_seq, nseq, D_MODEL), i, spool, ssc, scf, mix, ns=ns)
        hs = hs.reshape(dec_seq * nseq, D_MODEL)
        new_p.append(st_p)
        new_s.append(st_s)
        wg, wu, wd, wpg, wpp = f2_w
        f2 = [whole(wg), whole(wu), whole(wd), vec(g_f2_post, i), whole(wpg), whole(wpp), vec(g_ple_post, i), gfin]
        final = i == depth - 1
        nxt = [] if final else [(w_f1_gate, i + 1, g_f1_pre), (w_f1_up, i + 1, g_f1_pre), (w_f1_down, i + 1, None)]
        hp, *f1_w = _ffn_ple_call(hp, pp, i, f2, nxt, tm=tm_tok, final=final)
        hs, = _ffn_ple_call(hs, ps, i, f2, tm=tm, final=final)
    y_sample = hs.reshape(dec_seq, nseq, D_MODEL).transpose(1, 0, 2)
    stack_p = lambda k: jnp.stack([s[k] for s in new_p])
    stack_s = lambda k: jnp.stack([s[k] for s in new_s]).transpose(0, 2, 1, 3)
    return (hp.reshape(batch, seq, D_MODEL), y_sample,
            stack_p(0), stack_p(1), stack_p(2), stack_s(0), stack_s(1), stack_s(2))
```
